```python
import jax, jax.numpy as jnp
from jax import lax
import numpy as np

D_MODEL = 1024
BATCH = 32
SEQ = 2048
DEPTH = 1

CHUNK = 64
N_MEM = 256
EPS = 1e-6

HG_EXPAND = 128
HG_HEADS = D_MODEL // HG_EXPAND
HG_DK = HG_EXPAND
HG_DV = D_MODEL // HG_HEADS
HG_WIDTH = HG_HEADS * HG_DV
HG_CHUNK = 32

SSM_EXPAND = 2
SSM_INNER = SSM_EXPAND * D_MODEL
SSM_HEADDIM = 64
SSM_HEADS = SSM_INNER // SSM_HEADDIM
SSM_STATE = 128
SSM_GROUPS = 8
SSM_HPG = SSM_HEADS // SSM_GROUPS
SSM_CONV = 4
SSM_CONV_DIM = SSM_INNER + 2 * SSM_GROUPS * SSM_STATE
SSM_NORM_GROUP = SSM_INNER // SSM_GROUPS

XA_HEADS = 4
XA_HEADDIM = D_MODEL // XA_HEADS
XA_WIDTH = XA_HEADS * XA_HEADDIM

N_BRANCH = 3
FFN_HIDDEN = ((8 * D_MODEL // 3 + 255) // 256) * 256

IN_SIZES = (HG_WIDTH, HG_WIDTH, HG_WIDTH, HG_WIDTH,
            SSM_INNER, SSM_CONV_DIM, SSM_HEADS,
            XA_WIDTH,
            N_BRANCH * D_MODEL)
IN_COLS = sum(IN_SIZES)

kernel_name = "hybrid_hgrn2_ssd_memxattn_block"


def rmsnorm(x, w):
    xf = x.astype(jnp.float32)
    y = xf * lax.rsqrt(jnp.mean(xf * xf, axis=-1, keepdims=True) + EPS)
    return (y * w.astype(jnp.float32)).astype(x.dtype)


def hgrn2_mixer(q, f_logit, i, og, lb, gn_w):
    f32 = jnp.float32
    bsz, s, _ = q.shape
    L = HG_CHUNK
    n = s // L
    q = jax.nn.silu(q.astype(f32))
    f = lb + (1.0 - lb) * jax.nn.sigmoid(f_logit.astype(f32))
    k = 1.0 - f
    g = jnp.log(f)
    shp = (bsz, n, L, HG_HEADS, HG_DK)
    q = q.reshape(shp)
    k = k.reshape(shp)
    g = g.reshape(shp)
    v = i.astype(f32).reshape(bsz, n, L, HG_HEADS, HG_DV)
    b = jnp.cumsum(g, axis=2)
    b_ref = b[:, :, L // 2:L // 2 + 1]
    b_last = b[:, :, -1:]
    qr = q * jnp.exp(b - b_ref)
    kr = k * jnp.exp(b_ref - b)
    A = jnp.einsum('bntzd,bnszd->bnzts', qr, kr)
    causal = jnp.tril(jnp.ones((L, L), dtype=bool))
    A = jnp.where(causal, A, 0.0)
    o_intra = jnp.einsum('bnzts,bnszv->bntzv', A, v)
    qi = q * jnp.exp(b)
    ki = k * jnp.exp(b_last - b)
    dchunk = jnp.exp(b_last[:, :, 0])

    def step(state, inp):
        qc, kc, vc, dc = inp
        o = jnp.einsum('btzd,bzdv->btzv', qc, state)
        state = dc[..., None] * state + jnp.einsum('bszd,bszv->bzdv', kc, vc)
        return state, o

    s0 = jnp.zeros((bsz, HG_HEADS, HG_DK, HG_DV), f32)
    _, o_inter = lax.scan(step, s0, (qi.swapaxes(0, 1), ki.swapaxes(0, 1),
                                     v.swapaxes(0, 1), dchunk.swapaxes(0, 1)))
    o = o_intra + o_inter.swapaxes(0, 1)
    o = o * lax.rsqrt(jnp.mean(o * o, axis=-1, keepdims=True) + EPS)
    o = o.reshape(bsz, s, HG_WIDTH) * gn_w.astype(f32)
    return o * jax.nn.silu(og.astype(f32))


def mamba2_mixer(z, xbc, dt, conv_w, conv_b, dt_bias, a_log, d_skip, norm_w):
    f32 = jnp.float32
    bsz, s, _ = z.shape
    L = CHUNK
    n = s // L
    conv = lax.conv_general_dilated(
        xbc.astype(f32), conv_w.astype(f32)[:, None, :], window_strides=(1,),
        padding=[(SSM_CONV - 1, 0)], dimension_numbers=('NWC', 'WIO', 'NWC'),
        feature_group_count=SSM_CONV_DIM)
    xbc = jax.nn.silu(conv + conv_b.astype(f32))
    xs, bm, cm = jnp.split(xbc, [SSM_INNER, SSM_INNER + SSM_GROUPS * SSM_STATE], axis=-1)
    xs = xs.reshape(bsz, n, L, SSM_GROUPS, SSM_HPG, SSM_HEADDIM)
    bm = bm.reshape(bsz, n, L, SSM_GROUPS, SSM_STATE)
    cm = cm.reshape(bsz, n, L, SSM_GROUPS, SSM_STATE)
    dt = jax.nn.softplus(dt.astype(f32) + dt_bias.astype(f32)).reshape(bsz, n, L, SSM_GROUPS, SSM_HPG)
    a = -jnp.exp(a_log.astype(f32)).reshape(SSM_GROUPS, SSM_HPG)
    cum = jnp.cumsum(dt * a, axis=2)
    X = xs * dt[..., None]
    causal = jnp.tril(jnp.ones((L, L), dtype=bool))
    seg = cum[:, :, :, None] - cum[:, :, None, :]
    ldec = jnp.exp(jnp.where(causal[:, :, None, None], seg, -jnp.inf))
    cb = jnp.einsum('bctgn,bcsgn->bctsg', cm, bm)
    y_diag = jnp.einsum('bctsg,bctsgh,bcsghp->bctghp', cb, ldec, X)
    din = jnp.exp(cum)
    dout = jnp.exp(cum[:, :, -1:] - cum)
    dchunk = jnp.exp(cum[:, :, -1])

    def step(state, inp):
        cc, bc, xc, di, do, dc = inp
        y = jnp.einsum('btgn,bghpn,btgh->btghp', cc, state, di)
        state = dc[..., None, None] * state + jnp.einsum('bsgn,bsgh,bsghp->bghpn', bc, do, xc)
        return state, y

    s0 = jnp.zeros((bsz, SSM_GROUPS, SSM_HPG, SSM_HEADDIM, SSM_STATE), f32)
    _, y_off = lax.scan(step, s0, (cm.swapaxes(0, 1), bm.swapaxes(0, 1), X.swapaxes(0, 1),
                                   din.swapaxes(0, 1), dout.swapaxes(0, 1), dchunk.swapaxes(0, 1)))
    y = y_diag + y_off.swapaxes(0, 1) + d_skip.astype(f32).reshape(SSM_GROUPS, SSM_HPG)[..., None] * xs
    y = y.reshape(bsz, s, SSM_INNER) * jax.nn.silu(z.astype(f32))
    y = y.reshape(bsz, s, SSM_GROUPS, SSM_NORM_GROUP)
    y = y * lax.rsqrt(jnp.mean(y * y, axis=-1, keepdims=True) + EPS)
    return y.reshape(bsz, s, SSM_INNER) * norm_w.astype(f32)


def memory_cross_attention(q, mem_n, w_mem_kv):
    bsz, s, _ = q.shape
    m = mem_n.shape[1]
    k, v = jnp.split(mem_n @ w_mem_kv, 2, axis=-1)
    q = q.reshape(bsz, s, XA_HEADS, XA_HEADDIM)
    k = k.reshape(bsz, m, XA_HEADS, XA_HEADDIM)
    v = v.reshape(bsz, m, XA_HEADS, XA_HEADDIM)
    scores = jnp.einsum('bshd,bmhd->bhsm', q, k).astype(jnp.float32) * (XA_HEADDIM ** -0.5)
    p = jax.nn.softmax(scores, axis=-1).astype(v.dtype)
    o = jnp.einsum('bhsm,bmhd->bshd', p, v)
    return o.reshape(bsz, s, XA_WIDTH)


def setup_inputs(seed: int = 0) -> dict:
    key = jax.random.key(seed)
    ks = jax.random.split(key, 24)
    f32 = jnp.float32

    def nrm(k, shape, scale):
        return jax.random.normal(k, shape, f32) * scale

    def gain(k, shape):
        return 1.0 + 0.05 * jax.random.normal(k, shape, f32)

    dt0 = jnp.exp(jax.random.uniform(ks[12], (DEPTH, SSM_HEADS), f32)
                  * (np.log(0.1) - np.log(0.001)) + np.log(0.001))
    dt_bias = dt0 + jnp.log(-jnp.expm1(-dt0))
    a_log = jnp.log(jax.random.uniform(ks[13], (DEPTH, SSM_HEADS), f32, 1.0, 16.0))
    return {
        "x": jax.random.normal(ks[0], (BATCH, SEQ, D_MODEL), f32),
        "mem": jax.random.normal(ks[1], (BATCH, N_MEM, D_MODEL), f32),
        "norm_mix_pre": gain(ks[2], (DEPTH, D_MODEL)),
        "norm_mix_post": gain(ks[3], (DEPTH, D_MODEL)),
        "norm_ffn_pre": gain(ks[4], (DEPTH, D_MODEL)),
        "norm_ffn_post": gain(ks[5], (DEPTH, D_MODEL)),
        "norm_mem": gain(ks[6], (DEPTH, D_MODEL)),
        "w_in": nrm(ks[7], (DEPTH, D_MODEL, IN_COLS), D_MODEL ** -0.5),
        "hg_lb_param": 1.0 + 0.1 * jax.random.normal(ks[8], (DEPTH + 1, HG_WIDTH), f32),
        "hg_norm_w": gain(ks[9], (DEPTH, HG_WIDTH)),
        "conv_w": nrm(ks[10], (DEPTH, SSM_CONV, SSM_CONV_DIM), SSM_CONV ** -0.5),
        "conv_b": nrm(ks[11], (DEPTH, SSM_CONV_DIM), 0.02),
        "dt_bias": dt_bias,
        "a_log": a_log,
        "d_skip": 1.0 + 0.1 * jax.random.normal(ks[14], (DEPTH, SSM_HEADS), f32),
        "ssm_norm_w": gain(ks[15], (DEPTH, SSM_INNER)),
        "w_mem_kv": nrm(ks[16], (DEPTH, D_MODEL, 2 * XA_WIDTH), D_MODEL ** -0.5),
        "w_br_hg": nrm(ks[17], (DEPTH, HG_WIDTH, D_MODEL), HG_WIDTH ** -0.5),
        "w_br_ssm": nrm(ks[18], (DEPTH, SSM_INNER, D_MODEL), SSM_INNER ** -0.5),
        "w_br_xa": nrm(ks[19], (DEPTH, XA_WIDTH, D_MODEL), XA_WIDTH ** -0.5),
        "w_out": nrm(ks[20], (DEPTH, D_MODEL, D_MODEL), D_MODEL ** -0.5),
        "w_ffn_up": nrm(ks[21], (DEPTH, D_MODEL, 2 * FFN_HIDDEN), D_MODEL ** -0.5),
        "w_ffn_down": nrm(ks[22], (DEPTH, FFN_HIDDEN, D_MODEL), FFN_HIDDEN ** -0.5),
    }


def reference(x, mem, norm_mix_pre, norm_mix_post, norm_ffn_pre, norm_ffn_post, norm_mem,
              w_in, hg_lb_param, hg_norm_w, conv_w, conv_b, dt_bias, a_log, d_skip,
              ssm_norm_w, w_mem_kv, w_br_hg, w_br_ssm, w_br_xa, w_out, w_ffn_up, w_ffn_down):
    bsz, s, _ = x.shape
    split_pts = list(np.cumsum(IN_SIZES)[:-1])
    lb_all = jnp.cumsum(jax.nn.softmax(hg_lb_param.astype(jnp.float32), axis=0), axis=0)
    for l in range(DEPTH):
        h = rmsnorm(x, norm_mix_pre[l])
        proj = h @ w_in[l]
        (hq, hf, hi, hog, sz, sxbc, sdt, xq, gate_logits) = jnp.split(proj, split_pts, axis=-1)
        y_hg = hgrn2_mixer(hq, hf, hi, hog, lb_all[l], hg_norm_w[l]).astype(x.dtype)
        y_ssm = mamba2_mixer(sz, sxbc, sdt, conv_w[l], conv_b[l], dt_bias[l], a_log[l],
                             d_skip[l], ssm_norm_w[l]).astype(x.dtype)
        mem_n = rmsnorm(mem, norm_mem[l])
        y_xa = memory_cross_attention(xq, mem_n, w_mem_kv[l])
        gates = jax.nn.sigmoid(gate_logits.astype(jnp.float32)).astype(x.dtype)
        gates = gates.reshape(bsz, s, N_BRANCH, D_MODEL)
        merged = (gates[:, :, 0] * (y_hg @ w_br_hg[l])
                  + gates[:, :, 1] * (y_ssm @ w_br_ssm[l])
                  + gates[:, :, 2] * (y_xa @ w_br_xa[l]))
        x = x + rmsnorm(merged @ w_out[l], norm_mix_post[l])
        h = rmsnorm(x, norm_ffn_pre[l])
        g, u = jnp.split(h @ w_ffn_up[l], 2, axis=-1)
        x = x + rmsnorm((jax.nn.silu(g) * u) @ w_ffn_down[l], norm_ffn_post[l])
    return x
```

```python
import functools

import jax
import jax.numpy as jnp
from jax import lax
from jax.experimental import pallas as pl
from jax.experimental.pallas import tpu as pltpu

F32 = jnp.float32
BF16 = jnp.bfloat16

EPS = 1e-6
D_MODEL = 1024
N_MEM = 256

HG_HEADS = 8
HG_DK = 128
HG_DV = 128
HG_WIDTH = HG_HEADS * HG_DV
HG_CHUNK = 32

SSM_INNER = 2048
SSM_HEADDIM = 64
SSM_HEADS = 32
SSM_STATE = 128
SSM_GROUPS = 8
SSM_HPG = 4
SSM_CONV = 4
SSM_BC = SSM_GROUPS * SSM_STATE
SSM_CONV_DIM = SSM_INNER + 2 * SSM_BC
SSM_GROUP_W = SSM_HPG * SSM_HEADDIM
HEAD_PAD = 128

XA_HEADS = 4
XA_HEADDIM = 256
XA_WIDTH = 1024

FFN_HIDDEN = 2816

VMEM_LIMIT_BYTES = 56 * 1024 * 1024

SEQ_TILE = 256
SSD_CHUNK = 64
ROW_TILE = 512


def _dot(a, b):
    return jnp.dot(a, b, preferred_element_type=F32)


def _dot_nt(a, b):
    return lax.dot_general(a, b, (((1,), (1,)), ((), ())), preferred_element_type=F32)


def _dot_tn(a, b):
    return lax.dot_general(a, b, (((0,), (0,)), ((), ())), preferred_element_type=F32)


def _dot_split(m_bf16, v_f32):
    hi = v_f32.astype(BF16)
    lo = (v_f32 - hi.astype(F32)).astype(BF16)
    return _dot(m_bf16, hi) + _dot(m_bf16, lo)


def _dot_split_rhs(v_f32, m_bf16):
    hi = v_f32.astype(BF16)
    lo = (v_f32 - hi.astype(F32)).astype(BF16)
    return _dot(hi, m_bf16) + _dot(lo, m_bf16)


def _rmsnorm(x, w):
    return x * lax.rsqrt(jnp.mean(x * x, axis=-1, keepdims=True) + EPS) * w


def _sigmoid(x):
    return 1.0 / (1.0 + jnp.exp(-x))


def _silu(x):
    return x * _sigmoid(x)


def _softplus(x):
    return jnp.maximum(x, 0.0) + jnp.log1p(jnp.exp(-jnp.abs(x)))


def _chunk_tri(n, chunk, lower):
    r = lax.broadcasted_iota(jnp.int32, (n, n), 0)
    c = lax.broadcasted_iota(jnp.int32, (n, n), 1)
    same = (r // chunk) == (c // chunk)
    tri = (c <= r) if lower else (r <= c)
    return jnp.where(same & tri, 1.0, 0.0).astype(BF16)


def _const_spec(shape):
    nd = len(shape)
    return pl.BlockSpec(shape, lambda *_: (0,) * nd, pipeline_mode=pl.Buffered(1))


def _params(n_grid):
    return pltpu.CompilerParams(dimension_semantics=("arbitrary",) * n_grid,
                                vmem_limit_bytes=VMEM_LIMIT_BYTES)


def _mem_kv_kernel(mem_ref, nw_ref, w_ref, k_ref, v_ref):
    mn = _rmsnorm(mem_ref[...], nw_ref[...]).astype(BF16)
    kv = _dot(mn, w_ref[...])
    k_ref[...] = kv[:, :XA_WIDTH].astype(BF16)
    v_ref[...] = kv[:, XA_WIDTH:].astype(BF16)


def _mem_kv(mem2d, norm_w, w_kv):
    rows = mem2d.shape[0]
    tile = min(ROW_TILE, rows)
    out = jax.ShapeDtypeStruct((rows, XA_WIDTH), BF16)
    return pl.pallas_call(
        _mem_kv_kernel,
        grid=(rows // tile,),
        in_specs=[pl.BlockSpec((tile, D_MODEL), lambda i: (i, 0)),
                  _const_spec((1, D_MODEL)),
                  _const_spec((D_MODEL, 2 * XA_WIDTH))],
        out_specs=[pl.BlockSpec((tile, XA_WIDTH), lambda i: (i, 0))] * 2,
        out_shape=[out, out],
        compiler_params=_params(1),
        name="mem_kv",
    )(mem2d, norm_w, w_kv)


def _hgrn2_kernel(layer, x_ref, nw_ref, w_ref, wg_ref, lbp_ref, gnw_ref, wbr_ref, out_ref,
                  qr_s, kr_s, qi_s, ki_s, v_s, o_s, dch_s, state_s):
    ts = x_ref.shape[1]
    n_chunk = ts // HG_CHUNK

    @pl.when(pl.program_id(1) == 0)
    def _():
        state_s[...] = jnp.zeros_like(state_s)

    h = _rmsnorm(x_ref[0], nw_ref[...]).astype(BF16)

    lbp = lbp_ref[...]
    lbe = jnp.exp(lbp - jnp.max(lbp, axis=0, keepdims=True))
    lb = jnp.sum(lbe[0:layer + 1], axis=0, keepdims=True) / jnp.sum(lbe, axis=0, keepdims=True)

    f = lb + (1.0 - lb) * _sigmoid(_dot(h, w_ref[:, HG_WIDTH:2 * HG_WIDTH]))
    k = 1.0 - f
    g = jnp.log(f)
    b = _dot_split(_chunk_tri(ts, HG_CHUNK, True), g)
    b3 = b.reshape(n_chunk, HG_CHUNK, HG_WIDTH)
    b_ref = b3[:, HG_CHUNK // 2:HG_CHUNK // 2 + 1]
    b_last = b3[:, HG_CHUNK - 1:HG_CHUNK]
    k3 = k.reshape(n_chunk, HG_CHUNK, HG_WIDTH)
    kr_s[...] = (k3 * jnp.exp(b_ref - b3)).reshape(ts, HG_WIDTH).astype(BF16)
    ki_s[...] = (k3 * jnp.exp(b_last - b3)).reshape(ts, HG_WIDTH).astype(BF16)
    dch_s[...] = jnp.exp(b_last).reshape(n_chunk, HG_WIDTH)
    q3 = _silu(_dot(h, w_ref[:, 0:HG_WIDTH])).reshape(n_chunk, HG_CHUNK, HG_WIDTH)
    qr_s[...] = (q3 * jnp.exp(b3 - b_ref)).reshape(ts, HG_WIDTH).astype(BF16)
    qi_s[...] = (q3 * jnp.exp(b3)).reshape(ts, HG_WIDTH).astype(BF16)
    v_s[...] = _dot(h, w_ref[:, 2 * HG_WIDTH:3 * HG_WIDTH]).astype(BF16)

    r = lax.broadcasted_iota(jnp.int32, (ts, ts), 0)
    c = lax.broadcasted_iota(jnp.int32, (ts, ts), 1)
    causal = ((r // HG_CHUNK) == (c // HG_CHUNK)) & (c <= r)

    for z in range(HG_HEADS):
        hs = slice(z * HG_DK, (z + 1) * HG_DK)
        a = jnp.where(causal, _dot_nt(qr_s[:, hs], kr_s[:, hs]), 0.0)
        o_s[:, hs] = _dot(a.astype(BF16), v_s[:, hs])
        st = state_s[z]
        for ci in range(n_chunk):
            rs = slice(ci * HG_CHUNK, (ci + 1) * HG_CHUNK)
            o_s[rs, hs] += _dot_nt(qi_s[rs, hs], st.astype(BF16))
            st = dch_s[ci:ci + 1, hs] * st + _dot_tn(v_s[rs, hs], ki_s[rs, hs])
        state_s[z] = st

    og = _dot(h, w_ref[:, 3 * HG_WIDTH:4 * HG_WIDTH])
    gnw = gnw_ref[...]
    for z in range(HG_HEADS):
        hs = slice(z * HG_DK, (z + 1) * HG_DK)
        o = o_s[:, hs]
        o_s[:, hs] = o * lax.rsqrt(jnp.mean(o * o, axis=-1, keepdims=True) + EPS) * gnw[:, hs]
    y = (o_s[...] * _silu(og)).astype(BF16)
    gate = _sigmoid(_dot(h, wg_ref[...]))
    out_ref[0] = gate * _dot(y, wbr_ref[...])


def _hgrn2_branch(layer, x, norm_w, w_hg, w_gate, lb_param, gn_w, w_br):
    bsz, s, _ = x.shape
    ts = min(SEQ_TILE, s)
    tile = lambda: pl.BlockSpec((1, ts, D_MODEL), lambda b, i: (b, i, 0))
    return pl.pallas_call(
        functools.partial(_hgrn2_kernel, layer),
        grid=(bsz, s // ts),
        in_specs=[tile(),
                  _const_spec((1, D_MODEL)),
                  _const_spec((D_MODEL, 4 * HG_WIDTH)),
                  _const_spec((D_MODEL, D_MODEL)),
                  _const_spec(lb_param.shape),
                  _const_spec((1, HG_WIDTH)),
                  _const_spec((HG_WIDTH, D_MODEL))],
        out_specs=tile(),
        out_shape=jax.ShapeDtypeStruct(x.shape, F32),
        scratch_shapes=[pltpu.VMEM((ts, HG_WIDTH), BF16)] * 5
                       + [pltpu.VMEM((ts, HG_WIDTH), F32),
                          pltpu.VMEM((ts // HG_CHUNK, HG_WIDTH), F32),
                          pltpu.VMEM((HG_HEADS, HG_DV, HG_DK), F32)],
        compiler_params=_params(2),
        name="hgrn2_branch",
    )(x, norm_w, w_hg, w_gate, lb_param, gn_w, w_br)


def _ssd_kernel(x_ref, nw_ref, wz_ref, wxbc_ref, wdt_ref, wdtT_ref, wg_ref, convw_ref, convb_ref,
                dtb_ref, dtbT_ref, a_ref, aT_ref, dskip_ref, expand_ref, normw_ref, wbr_ref, out_ref,
                raw_s, xs_s, xd_s, bm_s, cm_s, y_s, din_s, state_s):
    ts = x_ref.shape[1]
    lc = SSD_CHUNK
    n_chunk = ts // lc
    pad = 8

    @pl.when(pl.program_id(1) == 0)
    def _():
        state_s[...] = jnp.zeros_like(state_s)
        raw_s[0:pad, :] = jnp.zeros((pad, SSM_CONV_DIM), F32)

    h = _rmsnorm(x_ref[0], nw_ref[...]).astype(BF16)

    raw_s[pad:pad + ts, :] = _dot(h, wxbc_ref[...])
    conv = convb_ref[...] + convw_ref[SSM_CONV - 1:SSM_CONV, :] * raw_s[pad:pad + ts, :]
    for j in range(SSM_CONV - 1):
        off = pad - (SSM_CONV - 1) + j
        conv = conv + convw_ref[j:j + 1, :] * raw_s[off:off + ts, :]
    raw_s[0:pad, :] = raw_s[ts:ts + pad, :]
    xbc = _silu(conv)
    xs = xbc[:, :SSM_INNER]
    bm_s[...] = xbc[:, SSM_INNER:SSM_INNER + SSM_BC].astype(BF16)
    cm_s[...] = xbc[:, SSM_INNER + SSM_BC:].astype(BF16)

    a = -jnp.exp(a_ref[...])
    dt = _softplus(_dot(h, wdt_ref[...]) + dtb_ref[...])
    cum = _dot_split(_chunk_tri(ts, lc, True), dt * a)
    aT = -jnp.exp(aT_ref[...])
    dtT = _softplus(_dot_nt(wdtT_ref[...], h) + dtbT_ref[...])
    cumT = _dot_split_rhs(dtT * aT, _chunk_tri(ts, lc, False))

    cum3 = cum.reshape(n_chunk, lc, HEAD_PAD)
    cum_last = cum3[:, lc - 1:lc]
    dout = jnp.exp(cum_last - cum3).reshape(ts, HEAD_PAD)
    expand = expand_ref[...]
    din_s[...] = _dot_split_rhs(jnp.exp(cum), expand)
    dt_x = _dot_split_rhs(dt, expand)
    dtd_x = _dot_split_rhs(dt * dout, expand)
    xs_s[...] = (xs * dt_x).astype(BF16)
    xd_s[...] = (xs * dtd_x).astype(BF16)
    y_s[...] = dskip_ref[...] * xs

    r = lax.broadcasted_iota(jnp.int32, (lc, lc), 0)
    c = lax.broadcasted_iota(jnp.int32, (lc, lc), 1)
    causal = c <= r

    for gi in range(SSM_GROUPS):
        ns = slice(gi * SSM_STATE, (gi + 1) * SSM_STATE)
        ws = slice(gi * SSM_GROUP_W, (gi + 1) * SSM_GROUP_W)
        st = state_s[gi]
        for ci in range(n_chunk):
            rs = slice(ci * lc, (ci + 1) * lc)
            cmg = cm_s[rs, ns]
            bmg = bm_s[rs, ns]
            cb = _dot_nt(cmg, bmg)
            for j in range(SSM_HPG):
                hd = gi * SSM_HPG + j
                cs = slice(hd * SSM_HEADDIM, (hd + 1) * SSM_HEADDIM)
                seg = cum[rs, hd:hd + 1] - cumT[hd:hd + 1, rs]
                m = jnp.where(causal, cb * jnp.exp(seg), 0.0).astype(BF16)
                y_s[rs, cs] += _dot(m, xs_s[rs, cs])
            y_s[rs, ws] += _dot(cmg, st.astype(BF16)) * din_s[rs, ws]
            dchunk = din_s[ci * lc + lc - 1:ci * lc + lc, ws]
            st = dchunk * st + _dot_tn(bmg, xd_s[rs, ws])
        state_s[gi] = st

    y = y_s[...] * _silu(_dot(h, wz_ref[...]))
    normw = normw_ref[...]
    for gi in range(SSM_GROUPS):
        ws = slice(gi * SSM_GROUP_W, (gi + 1) * SSM_GROUP_W)
        yg = y[:, ws]
        y_s[:, ws] = yg * lax.rsqrt(jnp.mean(yg * yg, axis=-1, keepdims=True) + EPS) * normw[:, ws]
    gate = _sigmoid(_dot(h, wg_ref[...]))
    out_ref[0] = gate * _dot(y_s[...].astype(BF16), wbr_ref[...])


def _ssd_branch(x, norm_w, w_z, w_xbc, w_dt, w_dtT, w_gate, conv_w, conv_b, dt_bias, dt_biasT,
                a_log, a_logT, d_skip_x, expand, ssm_norm_w, w_br):
    bsz, s, _ = x.shape
    ts = min(SEQ_TILE, s)
    tile = lambda: pl.BlockSpec((1, ts, D_MODEL), lambda b, i: (b, i, 0))
    consts = [norm_w, w_z, w_xbc, w_dt, w_dtT, w_gate, conv_w, conv_b, dt_bias, dt_biasT,
              a_log, a_logT, d_skip_x, expand, ssm_norm_w, w_br]
    return pl.pallas_call(
        _ssd_kernel,
        grid=(bsz, s // ts),
        in_specs=[tile()] + [_const_spec(c.shape) for c in consts],
        out_specs=tile(),
        out_shape=jax.ShapeDtypeStruct(x.shape, F32),
        scratch_shapes=[pltpu.VMEM((ts + 8, SSM_CONV_DIM), F32),
                        pltpu.VMEM((ts, SSM_INNER), BF16),
                        pltpu.VMEM((ts, SSM_INNER), BF16),
                        pltpu.VMEM((ts, SSM_BC), BF16),
                        pltpu.VMEM((ts, SSM_BC), BF16),
                        pltpu.VMEM((ts, SSM_INNER), F32),
                        pltpu.VMEM((ts, SSM_INNER), F32),
                        pltpu.VMEM((SSM_GROUPS, SSM_STATE, SSM_GROUP_W), F32)],
        compiler_params=_params(2),
        name="ssd_branch",
    )(x, *consts)


def _xattn_kernel(x_ref, mhg_ref, mssm_ref, k_ref, v_ref, nw_ref, wq_ref, wg_ref, wbr_ref,
                  wout_ref, npost_ref, out_ref, o_s):
    x = x_ref[0]
    h = _rmsnorm(x, nw_ref[...]).astype(BF16)
    q = _dot(h, wq_ref[...]).astype(BF16)
    for z in range(XA_HEADS):
        hs = slice(z * XA_HEADDIM, (z + 1) * XA_HEADDIM)
        sc = _dot_nt(q[:, hs], k_ref[0, :, hs]) * (XA_HEADDIM ** -0.5)
        e = jnp.exp(sc - jnp.max(sc, axis=-1, keepdims=True))
        p = e / jnp.sum(e, axis=-1, keepdims=True)
        o_s[:, hs] = _dot(p.astype(BF16), v_ref[0, :, hs]).astype(BF16)
    gate = _sigmoid(_dot(h, wg_ref[...]))
    merged = mhg_ref[0] + mssm_ref[0] + gate * _dot(o_s[...], wbr_ref[...])
    out_ref[0] = x + _rmsnorm(_dot(merged.astype(BF16), wout_ref[...]), npost_ref[...])


def _xattn_merge(x, m_hg, m_ssm, k, v, norm_w, w_q, w_gate, w_br, w_out, norm_post):
    bsz, s, _ = x.shape
    ts = min(SEQ_TILE, s)
    tile = lambda: pl.BlockSpec((1, ts, D_MODEL), lambda b, i: (b, i, 0))
    kv = lambda: pl.BlockSpec((1, N_MEM, XA_WIDTH), lambda b, i: (b, 0, 0))
    sq = lambda: _const_spec((D_MODEL, D_MODEL))
    return pl.pallas_call(
        _xattn_kernel,
        grid=(bsz, s // ts),
        in_specs=[tile(), tile(), tile(), kv(), kv(), _const_spec((1, D_MODEL)),
                  sq(), sq(), sq(), sq(), _const_spec((1, D_MODEL))],
        out_specs=tile(),
        out_shape=jax.ShapeDtypeStruct(x.shape, F32),
        scratch_shapes=[pltpu.VMEM((ts, XA_WIDTH), BF16)],
        compiler_params=_params(2),
        name="xattn_merge",
    )(x, m_hg, m_ssm, k, v, norm_w, w_q, w_gate, w_br, w_out, norm_post)


def _ffn_kernel(x_ref, npre_ref, wup_ref, wdown_ref, npost_ref, out_ref):
    x = x_ref[...]
    h = _rmsnorm(x, npre_ref[...]).astype(BF16)
    g = _dot(h, wup_ref[:, :FFN_HIDDEN])
    u = _dot(h, wup_ref[:, FFN_HIDDEN:])
    act = (_silu(g) * u).astype(BF16)
    out_ref[...] = x + _rmsnorm(_dot(act, wdown_ref[...]), npost_ref[...])


def _ffn(x2d, norm_pre, w_up, w_down, norm_post):
    rows = x2d.shape[0]
    tile = min(ROW_TILE, rows)
    spec = lambda: pl.BlockSpec((tile, D_MODEL), lambda i: (i, 0))
    return pl.pallas_call(
        _ffn_kernel,
        grid=(rows // tile,),
        in_specs=[spec(), _const_spec((1, D_MODEL)), _const_spec((D_MODEL, 2 * FFN_HIDDEN)),
                  _const_spec((FFN_HIDDEN, D_MODEL)), _const_spec((1, D_MODEL))],
        out_specs=spec(),
        out_shape=jax.ShapeDtypeStruct(x2d.shape, F32),
        compiler_params=_params(1),
        name="swiglu_ffn",
    )(x2d, norm_pre, w_up, w_down, norm_post)


def _pad_heads(v, fill=0.0):
    row = jnp.full((1, HEAD_PAD), fill, F32).at[0, :SSM_HEADS].set(v.astype(F32))
    return row, row.reshape(HEAD_PAD, 1)


def kernel(x, mem, norm_mix_pre, norm_mix_post, norm_ffn_pre, norm_ffn_post, norm_mem, w_in,
           hg_lb_param, hg_norm_w, conv_w, conv_b, dt_bias, a_log, d_skip, ssm_norm_w, w_mem_kv,
           w_br_hg, w_br_ssm, w_br_xa, w_out, w_ffn_up, w_ffn_down):
    bsz, s, d = x.shape
    depth = w_in.shape[0]
    assert d == D_MODEL and s % min(SEQ_TILE, s) == 0 and min(SEQ_TILE, s) % SSD_CHUNK == 0
    row = lambda v: v.reshape(1, -1).astype(F32)

    expand = (jnp.arange(HEAD_PAD)[:, None] == (jnp.arange(SSM_INNER)[None, :] // SSM_HEADDIM)).astype(BF16)

    o_hg = 4 * HG_WIDTH
    o_z, o_xbc = o_hg, o_hg + SSM_INNER
    o_dt = o_xbc + SSM_CONV_DIM
    o_q = o_dt + SSM_HEADS
    o_gate = o_q + XA_WIDTH

    for l in range(depth):
        w = w_in[l]
        w_hg = w[:, :o_hg].astype(BF16)
        w_z = w[:, o_z:o_xbc].astype(BF16)
        w_xbc = w[:, o_xbc:o_dt].astype(BF16)
        w_dt = jnp.zeros((D_MODEL, HEAD_PAD), BF16).at[:, :SSM_HEADS].set(w[:, o_dt:o_q].astype(BF16))
        w_q = w[:, o_q:o_gate].astype(BF16)
        w_g = [w[:, o_gate + i * D_MODEL:o_gate + (i + 1) * D_MODEL].astype(BF16) for i in range(3)]
        dtb, dtbT = _pad_heads(dt_bias[l])
        alog, alogT = _pad_heads(a_log[l])
        d_skip_x = jnp.repeat(d_skip[l].astype(F32), SSM_HEADDIM).reshape(1, SSM_INNER)
        n_pre = row(norm_mix_pre[l])

        k, v = _mem_kv(mem.reshape(-1, D_MODEL), row(norm_mem[l]), w_mem_kv[l].astype(BF16))
        k = k.reshape(bsz, -1, XA_WIDTH)
        v = v.reshape(bsz, -1, XA_WIDTH)
        m_hg = _hgrn2_branch(l, x, n_pre, w_hg, w_g[0], hg_lb_param.astype(F32),
                             row(hg_norm_w[l]), w_br_hg[l].astype(BF16))
        m_ssm = _ssd_branch(x, n_pre, w_z, w_xbc, w_dt, w_dt.T, w_g[1], conv_w[l].astype(F32),
                            row(conv_b[l]), dtb, dtbT, alog, alogT, d_skip_x, expand,
                            row(ssm_norm_w[l]), w_br_ssm[l].astype(BF16))
        x = _xattn_merge(x, m_hg, m_ssm, k, v, n_pre, w_q, w_g[2], w_br_xa[l].astype(BF16),
                         w_out[l].astype(BF16), row(norm_mix_post[l]))
        x = _ffn(x.reshape(-1, D_MODEL), row(norm_ffn_pre[l]), w_ffn_up[l].astype(BF16),
                 w_ffn_down[l].astype(BF16), row(norm_ffn_post[l])).reshape(bsz, s, d)
    return x
```

```python
import functools

import jax
import jax.numpy as jnp
from jax import lax
from jax.experimental import pallas as pl
from jax.experimental.pallas import tpu as pltpu

F32 = jnp.float32
BF16 = jnp.bfloat16

EPS = 1e-6
D_MODEL = 1024
N_MEM = 256

HG_HEADS = 8
HG_DK = 128
HG_DV = 128
HG_WIDTH = HG_HEADS * HG_DV
HG_CHUNK = 32

SSM_INNER = 2048
SSM_HEADDIM = 64
SSM_HEADS = 32
SSM_STATE = 128
SSM_GROUPS = 8
SSM_HPG = 4
SSM_CONV = 4
SSM_BC = SSM_GROUPS * SSM_STATE
SSM_CONV_DIM = SSM_INNER + 2 * SSM_BC
SSM_GROUP_W = SSM_HPG * SSM_HEADDIM
LANES = 128
HEAD_PAD = LANES

XA_HEADS = 4
XA_HEADDIM = 256
XA_WIDTH = 1024

FFN_HIDDEN = 2816

VMEM_LIMIT_BYTES = 56 * 1024 * 1024

SEQ_TILE = 256
SSD_CHUNK = 128
ROW_TILE = 512


def _dot(a, b):
    return jnp.dot(a, b, preferred_element_type=F32)


def _dot_nt(a, b):
    return lax.dot_general(a, b, (((1,), (1,)), ((), ())), preferred_element_type=F32)


def _dot_tn(a, b):
    return lax.dot_general(a, b, (((0,), (0,)), ((), ())), preferred_element_type=F32)


def _dot_split(m_bf16, v_f32):
    hi = v_f32.astype(BF16)
    lo = (v_f32 - hi.astype(F32)).astype(BF16)
    return _dot(m_bf16, hi) + _dot(m_bf16, lo)


def _dot_split_rhs(v_f32, m_bf16):
    hi = v_f32.astype(BF16)
    lo = (v_f32 - hi.astype(F32)).astype(BF16)
    return _dot(hi, m_bf16) + _dot(lo, m_bf16)


def _rmsnorm(x, w):
    return x * lax.rsqrt(jnp.mean(x * x, axis=-1, keepdims=True) + EPS) * w


def _sigmoid(x):
    return 1.0 / (1.0 + jnp.exp(-x))


def _silu(x):
    return x * _sigmoid(x)


def _softplus(x):
    return jnp.maximum(x, 0.0) + jnp.log1p(jnp.exp(-jnp.abs(x)))


def _chunk_tri(n, chunk, lower):
    r = lax.broadcasted_iota(jnp.int32, (n, n), 0)
    c = lax.broadcasted_iota(jnp.int32, (n, n), 1)
    same = (r // chunk) == (c // chunk)
    tri = (c <= r) if lower else (r <= c)
    return jnp.where(same & tri, 1.0, 0.0).astype(BF16)


def _const_spec(shape):
    nd = len(shape)
    return pl.BlockSpec(shape, lambda *_: (0,) * nd, pipeline_mode=pl.Buffered(1))


def _params(n_grid):
    return pltpu.CompilerParams(dimension_semantics=("arbitrary",) * n_grid,
                                vmem_limit_bytes=VMEM_LIMIT_BYTES)


def _mem_kv_kernel(mem_ref, nw_ref, w_ref, k_ref, v_ref):
    mn = _rmsnorm(mem_ref[...], nw_ref[...]).astype(BF16)
    kv = _dot(mn, w_ref[...])
    k_ref[...] = kv[:, :XA_WIDTH].astype(BF16)
    v_ref[...] = kv[:, XA_WIDTH:].astype(BF16)


def _mem_kv(mem2d, norm_w, w_kv):
    rows = mem2d.shape[0]
    tile = min(ROW_TILE, rows)
    out = jax.ShapeDtypeStruct((rows, XA_WIDTH), BF16)
    return pl.pallas_call(
        _mem_kv_kernel,
        grid=(rows // tile,),
        in_specs=[pl.BlockSpec((tile, D_MODEL), lambda i: (i, 0)),
                  _const_spec((1, D_MODEL)),
                  _const_spec((D_MODEL, 2 * XA_WIDTH))],
        out_specs=[pl.BlockSpec((tile, XA_WIDTH), lambda i: (i, 0))] * 2,
        out_shape=[out, out],
        compiler_params=_params(1),
        name="mem_kv",
    )(mem2d, norm_w, w_kv)


def _hgrn2_kernel(layer, x_ref, nw_ref, w_ref, wg_ref, lbp_ref, gnw_ref, wbr_ref, out_ref,
                  qr_s, kr_s, qi_s, ki_s, v_s, y_s, o_s, dch_s, state_s):
    ts = x_ref.shape[1]
    n_chunk = ts // HG_CHUNK
    assert n_chunk % 2 == 0

    @pl.when(pl.program_id(1) == 0)
    def _():
        state_s[...] = jnp.zeros_like(state_s)

    h = _rmsnorm(x_ref[0], nw_ref[...]).astype(BF16)

    lbp = lbp_ref[...]
    lbe = jnp.exp(lbp - jnp.max(lbp, axis=0, keepdims=True))
    lb = jnp.sum(lbe[0:layer + 1], axis=0, keepdims=True) / jnp.sum(lbe, axis=0, keepdims=True)

    f = lb + (1.0 - lb) * _sigmoid(_dot(h, w_ref[:, HG_WIDTH:2 * HG_WIDTH]))
    k = 1.0 - f
    g = jnp.log(f)
    b = _dot_split(_chunk_tri(ts, HG_CHUNK, True), g)
    b3 = b.reshape(n_chunk, HG_CHUNK, HG_WIDTH)
    b_ref = b3[:, HG_CHUNK // 2:HG_CHUNK // 2 + 1]
    b_last = b3[:, HG_CHUNK - 1:HG_CHUNK]
    k3 = k.reshape(n_chunk, HG_CHUNK, HG_WIDTH)
    kr_s[...] = (k3 * jnp.exp(b_ref - b3)).reshape(ts, HG_WIDTH).astype(BF16)
    ki_s[...] = (k3 * jnp.exp(b_last - b3)).reshape(ts, HG_WIDTH).astype(BF16)
    dch_s[...] = jnp.exp(b_last).reshape(n_chunk, HG_WIDTH)
    q3 = _silu(_dot(h, w_ref[:, 0:HG_WIDTH])).reshape(n_chunk, HG_CHUNK, HG_WIDTH)
    qr_s[...] = (q3 * jnp.exp(b3 - b_ref)).reshape(ts, HG_WIDTH).astype(BF16)
    qi_s[...] = (q3 * jnp.exp(b3)).reshape(ts, HG_WIDTH).astype(BF16)
    v_s[...] = _dot(h, w_ref[:, 2 * HG_WIDTH:3 * HG_WIDTH]).astype(BF16)

    r = lax.broadcasted_iota(jnp.int32, (ts, ts), 0)
    c = lax.broadcasted_iota(jnp.int32, (ts, ts), 1)
    causal = ((r // HG_CHUNK) == (c // HG_CHUNK)) & (c <= r)

    o_s[...] = _silu(_dot(h, w_ref[:, 3 * HG_WIDTH:4 * HG_WIDTH])) * gnw_ref[...]

    zeros = jnp.zeros((HG_CHUNK, HG_DK), BF16)
    for z in range(HG_HEADS):
        hs = slice(z * HG_DK, (z + 1) * HG_DK)
        a = jnp.where(causal, _dot_nt(qr_s[:, hs], kr_s[:, hs]), 0.0)
        o_intra = _dot(a.astype(BF16), v_s[:, hs])
        incs = []
        for pi in range(n_chunk // 2):
            ra = slice(2 * pi * HG_CHUNK, (2 * pi + 1) * HG_CHUNK)
            rb = slice((2 * pi + 1) * HG_CHUNK, (2 * pi + 2) * HG_CHUNK)
            kib = jnp.concatenate([jnp.concatenate([ki_s[ra, hs], zeros], axis=1),
                                   jnp.concatenate([zeros, ki_s[rb, hs]], axis=1)], axis=0)
            inc = _dot_tn(v_s[2 * pi * HG_CHUNK:(2 * pi + 2) * HG_CHUNK, hs], kib)
            incs += [inc[:, :HG_DK], inc[:, HG_DK:]]
        st = state_s[z]
        states = []
        for ci in range(n_chunk):
            states.append(st.astype(BF16))
            st = dch_s[ci:ci + 1, hs] * st + incs[ci]
        state_s[z] = st
        o_inter = jnp.concatenate(
            [_dot_nt(qi_s[ci * HG_CHUNK:(ci + 1) * HG_CHUNK, hs], states[ci]) for ci in range(n_chunk)], axis=0)
        o = o_intra + o_inter
        y_s[:, hs] = (o * lax.rsqrt(jnp.mean(o * o, axis=-1, keepdims=True) + EPS) * o_s[:, hs]).astype(BF16)
    gate = _sigmoid(_dot(h, wg_ref[...]))
    out_ref[0] = gate * _dot(y_s[...], wbr_ref[...])


def _hgrn2_branch(layer, x, norm_w, w_hg, w_gate, lb_param, gn_w, w_br):
    bsz, s, _ = x.shape
    ts = min(SEQ_TILE, s)
    tile = lambda: pl.BlockSpec((1, ts, D_MODEL), lambda b, i: (b, i, 0))
    return pl.pallas_call(
        functools.partial(_hgrn2_kernel, layer),
        grid=(bsz, s // ts),
        in_specs=[tile(),
                  _const_spec((1, D_MODEL)),
                  _const_spec((D_MODEL, 4 * HG_WIDTH)),
                  _const_spec((D_MODEL, D_MODEL)),
                  _const_spec(lb_param.shape),
                  _const_spec((1, HG_WIDTH)),
                  _const_spec((HG_WIDTH, D_MODEL))],
        out_specs=tile(),
        out_shape=jax.ShapeDtypeStruct(x.shape, F32),
        scratch_shapes=[pltpu.VMEM((ts, HG_WIDTH), BF16)] * 6
                       + [pltpu.VMEM((ts, HG_WIDTH), F32),
                          pltpu.VMEM((ts // HG_CHUNK, HG_WIDTH), F32),
                          pltpu.VMEM((HG_HEADS, HG_DV, HG_DK), F32)],
        compiler_params=_params(2),
        name="hgrn2_branch",
    )(x, norm_w, w_hg, w_gate, lb_param, gn_w, w_br)


def _expand_heads(v, expand2):
    hi = v.astype(BF16)
    lo = (v - hi.astype(F32)).astype(BF16)
    return _dot(jnp.concatenate([hi, lo], axis=1), expand2)


def _ssd_kernel(x_ref, nw_ref, wz_ref, wxbc_ref, wdt_ref, wdtT_ref, wg_ref, convw_ref, convb_ref,
                dtb_ref, dtbT_ref, a_ref, aT_ref, dskip_ref, expand2_ref, normw_ref, wbr_ref, out_ref,
                raw_s, xm_s, xd_s, bm_s, cm_s, y_s, din_s, state_s):
    ts = x_ref.shape[1]
    lc = SSD_CHUNK
    n_chunk = ts // lc
    pad = 8
    n_lane_tiles = SSM_CONV_DIM // LANES

    @pl.when(pl.program_id(1) == 0)
    def _():
        state_s[...] = jnp.zeros_like(state_s)
        raw_s[:, 0:pad, :] = jnp.zeros((n_lane_tiles, pad, LANES), F32)

    h = _rmsnorm(x_ref[0], nw_ref[...]).astype(BF16)

    raw = _dot(h, wxbc_ref[...])
    cols = []
    for ct in range(n_lane_tiles):
        ls = slice(ct * LANES, (ct + 1) * LANES)
        raw_s[ct, pad:pad + ts, :] = raw[:, ls]
        acc = convb_ref[:, ls] + convw_ref[SSM_CONV - 1:SSM_CONV, ls] * raw[:, ls]
        for j in range(SSM_CONV - 1):
            off = pad - (SSM_CONV - 1) + j
            acc = acc + convw_ref[j:j + 1, ls] * raw_s[ct, off:off + ts, :]
        raw_s[ct, 0:pad, :] = raw_s[ct, ts:ts + pad, :]
        cols.append(acc)
    xbc = _silu(jnp.concatenate(cols, axis=1))
    xs = xbc[:, :SSM_INNER]
    bm_s[...] = xbc[:, SSM_INNER:SSM_INNER + SSM_BC].astype(BF16)
    cm_s[...] = xbc[:, SSM_INNER + SSM_BC:].astype(BF16)

    a = -jnp.exp(a_ref[...])
    dt = _softplus(_dot(h, wdt_ref[...]) + dtb_ref[...])
    cum = _dot_split(_chunk_tri(ts, lc, True), dt * a)
    aT = -jnp.exp(aT_ref[...])
    dtT = _softplus(_dot_nt(wdtT_ref[...], h) + dtbT_ref[...])
    cumT = _dot_split_rhs(dtT * aT, _chunk_tri(ts, lc, False))

    cum3 = cum.reshape(n_chunk, lc, HEAD_PAD)
    cum_last = cum3[:, lc - 1:lc]
    dout = jnp.exp(cum_last - cum3).reshape(ts, HEAD_PAD)
    expand2 = expand2_ref[...]
    din_s[...] = _expand_heads(jnp.exp(cum), expand2)
    xd_s[...] = (xs * _expand_heads(dt * dout, expand2)).astype(BF16)
    y_s[...] = dskip_ref[...] * xs
    xs_bf = xs.astype(BF16)
    head_in_group = (lax.broadcasted_iota(jnp.int32, (1, SSM_INNER), 1) // SSM_HEADDIM) % SSM_HPG
    for j in range(SSM_HPG):
        xm_s[j] = jnp.where(head_in_group == j, xs_bf, jnp.zeros_like(xs_bf))

    r = lax.broadcasted_iota(jnp.int32, (lc, lc), 0)
    c = lax.broadcasted_iota(jnp.int32, (lc, lc), 1)
    causal = c <= r

    for gi in range(SSM_GROUPS):
        ns = slice(gi * SSM_STATE, (gi + 1) * SSM_STATE)
        ws = slice(gi * SSM_GROUP_W, (gi + 1) * SSM_GROUP_W)
        chunks = [slice(ci * lc, (ci + 1) * lc) for ci in range(n_chunk)]
        incs = [_dot_tn(bm_s[rs, ns], xd_s[rs, ws]) for rs in chunks]
        st = state_s[gi]
        states = []
        for ci in range(n_chunk):
            states.append(st.astype(BF16))
            dchunk = din_s[ci * lc + lc - 1:ci * lc + lc, ws]
            st = dchunk * st + incs[ci]
        state_s[gi] = st
        for ci, rs in enumerate(chunks):
            cmg = cm_s[rs, ns]
            cb = _dot_nt(cmg, bm_s[rs, ns])
            y = _dot(cmg, states[ci]) * din_s[rs, ws]
            for j in range(SSM_HPG):
                hd = gi * SSM_HPG + j
                seg = cum[rs, hd:hd + 1] - cumT[hd:hd + 1, rs]
                m = jnp.where(causal, cb * jnp.exp(seg) * dtT[hd:hd + 1, rs], 0.0).astype(BF16)
                y = y + _dot(m, xm_s[j, rs, ws])
            y_s[rs, ws] += y

    y = y_s[...] * _silu(_dot(h, wz_ref[...]))
    normw = normw_ref[...]
    for gi in range(SSM_GROUPS):
        ws = slice(gi * SSM_GROUP_W, (gi + 1) * SSM_GROUP_W)
        yg = y[:, ws]
        y_s[:, ws] = yg * lax.rsqrt(jnp.mean(yg * yg, axis=-1, keepdims=True) + EPS) * normw[:, ws]
    gate = _sigmoid(_dot(h, wg_ref[...]))
    out_ref[0] = gate * _dot(y_s[...].astype(BF16), wbr_ref[...])


def _ssd_branch(x, norm_w, w_z, w_xbc, w_dt, w_dtT, w_gate, conv_w, conv_b, dt_bias, dt_biasT,
                a_log, a_logT, d_skip_x, expand2, ssm_norm_w, w_br):
    bsz, s, _ = x.shape
    ts = min(SEQ_TILE, s)
    tile = lambda: pl.BlockSpec((1, ts, D_MODEL), lambda b, i: (b, i, 0))
    consts = [norm_w, w_z, w_xbc, w_dt, w_dtT, w_gate, conv_w, conv_b, dt_bias, dt_biasT,
              a_log, a_logT, d_skip_x, expand2, ssm_norm_w, w_br]
    return pl.pallas_call(
        _ssd_kernel,
        grid=(bsz, s // ts),
        in_specs=[tile()] + [_const_spec(c.shape) for c in consts],
        out_specs=tile(),
        out_shape=jax.ShapeDtypeStruct(x.shape, F32),
        scratch_shapes=[pltpu.VMEM((SSM_CONV_DIM // LANES, ts + 8, LANES), F32),
                        pltpu.VMEM((SSM_HPG, ts, SSM_INNER), BF16),
                        pltpu.VMEM((ts, SSM_INNER), BF16),
                        pltpu.VMEM((ts, SSM_BC), BF16),
                        pltpu.VMEM((ts, SSM_BC), BF16),
                        pltpu.VMEM((ts, SSM_INNER), F32),
                        pltpu.VMEM((ts, SSM_INNER), F32),
                        pltpu.VMEM((SSM_GROUPS, SSM_STATE, SSM_GROUP_W), F32)],
        compiler_params=_params(2),
        name="ssd_branch",
    )(x, *consts)


def _xattn_kernel(x_ref, mhg_ref, mssm_ref, k_ref, v_ref, nw_ref, wq_ref, wg_ref, wbr_ref,
                  wout_ref, npost_ref, out_ref, o_s):
    x = x_ref[0]
    h = _rmsnorm(x, nw_ref[...]).astype(BF16)
    q = _dot(h, wq_ref[...]).astype(BF16)
    for z in range(XA_HEADS):
        hs = slice(z * XA_HEADDIM, (z + 1) * XA_HEADDIM)
        sc = _dot_nt(q[:, hs], k_ref[0, :, hs]) * (XA_HEADDIM ** -0.5)
        e = jnp.exp(sc - jnp.max(sc, axis=-1, keepdims=True))
        p = e / jnp.sum(e, axis=-1, keepdims=True)
        o_s[:, hs] = _dot(p.astype(BF16), v_ref[0, :, hs]).astype(BF16)
    gate = _sigmoid(_dot(h, wg_ref[...]))
    merged = mhg_ref[0] + mssm_ref[0] + gate * _dot(o_s[...], wbr_ref[...])
    out_ref[0] = x + _rmsnorm(_dot(merged.astype(BF16), wout_ref[...]), npost_ref[...])


def _xattn_merge(x, m_hg, m_ssm, k, v, norm_w, w_q, w_gate, w_br, w_out, norm_post):
    bsz, s, _ = x.shape
    ts = min(SEQ_TILE, s)
    tile = lambda: pl.BlockSpec((1, ts, D_MODEL), lambda b, i: (b, i, 0))
    kv = lambda: pl.BlockSpec((1, N_MEM, XA_WIDTH), lambda b, i: (b, 0, 0))
    sq = lambda: _const_spec((D_MODEL, D_MODEL))
    return pl.pallas_call(
        _xattn_kernel,
        grid=(bsz, s // ts),
        in_specs=[tile(), tile(), tile(), kv(), kv(), _const_spec((1, D_MODEL)),
                  sq(), sq(), sq(), sq(), _const_spec((1, D_MODEL))],
        out_specs=tile(),
        out_shape=jax.ShapeDtypeStruct(x.shape, F32),
        scratch_shapes=[pltpu.VMEM((ts, XA_WIDTH), BF16)],
        compiler_params=_params(2),
        name="xattn_merge",
    )(x, m_hg, m_ssm, k, v, norm_w, w_q, w_gate, w_br, w_out, norm_post)


def _ffn_kernel(x_ref, npre_ref, wup_ref, wdown_ref, npost_ref, out_ref):
    x = x_ref[...]
    h = _rmsnorm(x, npre_ref[...]).astype(BF16)
    g = _dot(h, wup_ref[:, :FFN_HIDDEN])
    u = _dot(h, wup_ref[:, FFN_HIDDEN:])
    act = (_silu(g) * u).astype(BF16)
    out_ref[...] = x + _rmsnorm(_dot(act, wdown_ref[...]), npost_ref[...])


def _ffn(x2d, norm_pre, w_up, w_down, norm_post):
    rows = x2d.shape[0]
    tile = min(ROW_TILE, rows)
    spec = lambda: pl.BlockSpec((tile, D_MODEL), lambda i: (i, 0))
    return pl.pallas_call(
        _ffn_kernel,
        grid=(rows // tile,),
        in_specs=[spec(), _const_spec((1, D_MODEL)), _const_spec((D_MODEL, 2 * FFN_HIDDEN)),
                  _const_spec((FFN_HIDDEN, D_MODEL)), _const_spec((1, D_MODEL))],
        out_specs=spec(),
        out_shape=jax.ShapeDtypeStruct(x2d.shape, F32),
        compiler_params=_params(1),
        name="swiglu_ffn",
    )(x2d, norm_pre, w_up, w_down, norm_post)


def _pad_heads(v, fill=0.0):
    row = jnp.full((1, HEAD_PAD), fill, F32).at[0, :SSM_HEADS].set(v.astype(F32))
    return row, row.reshape(HEAD_PAD, 1)


def kernel(x, mem, norm_mix_pre, norm_mix_post, norm_ffn_pre, norm_ffn_post, norm_mem, w_in,
           hg_lb_param, hg_norm_w, conv_w, conv_b, dt_bias, a_log, d_skip, ssm_norm_w, w_mem_kv,
           w_br_hg, w_br_ssm, w_br_xa, w_out, w_ffn_up, w_ffn_down):
    bsz, s, d = x.shape
    depth = w_in.shape[0]
    assert d == D_MODEL and s % min(SEQ_TILE, s) == 0 and min(SEQ_TILE, s) % SSD_CHUNK == 0
    row = lambda v: v.reshape(1, -1).astype(F32)

    expand = (jnp.arange(HEAD_PAD)[:, None] == (jnp.arange(SSM_INNER)[None, :] // SSM_HEADDIM)).astype(BF16)
    expand2 = jnp.concatenate([expand, expand], axis=0)

    o_hg = 4 * HG_WIDTH
    o_z, o_xbc = o_hg, o_hg + SSM_INNER
    o_dt = o_xbc + SSM_CONV_DIM
    o_q = o_dt + SSM_HEADS
    o_gate = o_q + XA_WIDTH

    for l in range(depth):
        w = w_in[l]
        w_hg = w[:, :o_hg].astype(BF16)
        w_z = w[:, o_z:o_xbc].astype(BF16)
        w_xbc = w[:, o_xbc:o_dt].astype(BF16)
        w_dt = jnp.zeros((D_MODEL, HEAD_PAD), BF16).at[:, :SSM_HEADS].set(w[:, o_dt:o_q].astype(BF16))
        w_q = w[:, o_q:o_gate].astype(BF16)
        w_g = [w[:, o_gate + i * D_MODEL:o_gate + (i + 1) * D_MODEL].astype(BF16) for i in range(3)]
        dtb, dtbT = _pad_heads(dt_bias[l])
        alog, alogT = _pad_heads(a_log[l])
        d_skip_x = jnp.repeat(d_skip[l].astype(F32), SSM_HEADDIM).reshape(1, SSM_INNER)
        n_pre = row(norm_mix_pre[l])

        k, v = _mem_kv(mem.reshape(-1, D_MODEL), row(norm_mem[l]), w_mem_kv[l].astype(BF16))
        k = k.reshape(bsz, -1, XA_WIDTH)
        v = v.reshape(bsz, -1, XA_WIDTH)
        m_hg = _hgrn2_branch(l, x, n_pre, w_hg, w_g[0], hg_lb_param.astype(F32),
                             row(hg_norm_w[l]), w_br_hg[l].astype(BF16))
        m_ssm = _ssd_branch(x, n_pre, w_z, w_xbc, w_dt, w_dt.T, w_g[1], conv_w[l].astype(F32),
                            row(conv_b[l]), dtb, dtbT, alog, alogT, d_skip_x, expand2,
                            row(ssm_norm_w[l]), w_br_ssm[l].astype(BF16))
        x = _xattn_merge(x, m_hg, m_ssm, k, v, n_pre, w_q, w_g[2], w_br_xa[l].astype(BF16),
                         w_out[l].astype(BF16), row(norm_mix_post[l]))
        x = _ffn(x.reshape(-1, D_MODEL), row(norm_ffn_pre[l]), w_ffn_up[l].astype(BF16),
                 w_ffn_down[l].astype(BF16), row(norm_ffn_post[l])).reshape(bsz, s, d)
    return x
```

```python
import functools

import jax
import jax.numpy as jnp
from jax import lax
from jax.experimental import pallas as pl
from jax.experimental.pallas import tpu as pltpu

F32 = jnp.float32
BF16 = jnp.bfloat16

EPS = 1e-6
D_MODEL = 1024
N_MEM = 256

HG_HEADS = 8
HG_DK = 128
HG_DV = 128
HG_WIDTH = HG_HEADS * HG_DV
HG_CHUNK = 32

SSM_INNER = 2048
SSM_HEADDIM = 64
SSM_HEADS = 32
SSM_STATE = 128
SSM_GROUPS = 8
SSM_HPG = 4
SSM_CONV = 4
SSM_BC = SSM_GROUPS * SSM_STATE
SSM_CONV_DIM = SSM_INNER + 2 * SSM_BC
SSM_GROUP_W = SSM_HPG * SSM_HEADDIM
LANES = 128
HEAD_PAD = LANES

XA_HEADS = 4
XA_HEADDIM = 256
XA_WIDTH = 1024

FFN_HIDDEN = 2816

VMEM_LIMIT_BYTES = 56 * 1024 * 1024

HG_SEQ_TILE = 512
SSD_SEQ_TILE = 256
XA_SEQ_TILE = 512
ROW_TILE = 512
HG_SUBTILE = 256
HG_COL_BLOCK = 256
SSD_CHUNK = 128
CUMSUM_BLOCK = 128


def _dot(a, b):
    return jnp.dot(a, b, preferred_element_type=F32)


def _dot_nt(a, b):
    return lax.dot_general(a, b, (((1,), (1,)), ((), ())), preferred_element_type=F32)


def _dot_tn(a, b):
    return lax.dot_general(a, b, (((0,), (0,)), ((), ())), preferred_element_type=F32)


def _split_bf16(v):
    hi = v.astype(BF16)
    return hi, (v - hi.astype(F32)).astype(BF16)


def _rmsnorm(x, w):
    return x * lax.rsqrt(jnp.mean(x * x, axis=-1, keepdims=True) + EPS) * w


def _sigmoid(x):
    return 1.0 / (1.0 + jnp.exp(-x))


def _silu(x):
    return x * _sigmoid(x)


def _softplus(x):
    return jnp.maximum(x, 0.0) + jnp.log1p(jnp.exp(-jnp.abs(x)))


def _chunk_tri(n, chunk, lower):
    r = lax.broadcasted_iota(jnp.int32, (n, n), 0)
    c = lax.broadcasted_iota(jnp.int32, (n, n), 1)
    same = (r // chunk) == (c // chunk)
    tri = (c <= r) if lower else (r <= c)
    return jnp.where(same & tri, 1.0, 0.0).astype(BF16)


def _chunk_cumsum_rows(v, chunk):
    blk = CUMSUM_BLOCK
    tri = _chunk_tri(blk, chunk, True)
    tri2 = jnp.concatenate([tri, tri], axis=1)
    hi, lo = _split_bf16(v)
    return jnp.concatenate(
        [_dot(tri2, jnp.concatenate([hi[r0:r0 + blk], lo[r0:r0 + blk]], axis=0))
         for r0 in range(0, v.shape[0], blk)], axis=0)


def _chunk_cumsum_lanes(v, chunk):
    blk = CUMSUM_BLOCK
    tri = _chunk_tri(blk, chunk, False)
    tri2 = jnp.concatenate([tri, tri], axis=0)
    hi, lo = _split_bf16(v)
    return jnp.concatenate(
        [_dot(jnp.concatenate([hi[:, c0:c0 + blk], lo[:, c0:c0 + blk]], axis=1), tri2)
         for c0 in range(0, v.shape[1], blk)], axis=1)


def _const_spec(shape):
    nd = len(shape)
    return pl.BlockSpec(shape, lambda *_: (0,) * nd, pipeline_mode=pl.Buffered(1))


def _params(n_grid):
    return pltpu.CompilerParams(dimension_semantics=("arbitrary",) * n_grid,
                                vmem_limit_bytes=VMEM_LIMIT_BYTES)


def _mem_kv_kernel(mem_ref, nw_ref, w_ref, k_ref, v_ref):
    mn = _rmsnorm(mem_ref[...], nw_ref[...]).astype(BF16)
    kv = _dot(mn, w_ref[...])
    k_ref[...] = kv[:, :XA_WIDTH].astype(BF16)
    v_ref[...] = kv[:, XA_WIDTH:].astype(BF16)


def _mem_kv(mem2d, norm_w, w_kv):
    rows = mem2d.shape[0]
    tile = min(ROW_TILE, rows)
    out = jax.ShapeDtypeStruct((rows, XA_WIDTH), BF16)
    return pl.pallas_call(
        _mem_kv_kernel,
        grid=(rows // tile,),
        in_specs=[pl.BlockSpec((tile, D_MODEL), lambda i: (i, 0)),
                  _const_spec((1, D_MODEL)),
                  _const_spec((D_MODEL, 2 * XA_WIDTH))],
        out_specs=[pl.BlockSpec((tile, XA_WIDTH), lambda i: (i, 0))] * 2,
        out_shape=[out, out],
        compiler_params=_params(1),
        name="mem_kv",
    )(mem2d, norm_w, w_kv)


_DONE = object()


def _run_skewed(stage_generators):
    waiting = list(stage_generators)
    active = []
    while waiting or active:
        if waiting:
            active.append(waiting.pop(0))
        active = [g for g in active if next(g, _DONE) is not _DONE]


def _hgrn2_kernel(layer, x_ref, nw_ref, w_ref, wg_ref, lbp_ref, gnw_ref, wbr_ref, out_ref,
                  qr_s, kr_s, qi_s, ki_s, v_s, y_s, h_s, o_s, g_s, dch_s, state_s):
    ts = x_ref.shape[1]
    n_chunk = ts // HG_CHUNK
    sub = min(HG_SUBTILE, ts)
    width = HG_COL_BLOCK
    assert n_chunk % 2 == 0 and ts % sub == 0 and ts % CUMSUM_BLOCK == 0 and width % HG_DK == 0

    @pl.when(pl.program_id(1) == 0)
    def _():
        state_s[...] = jnp.zeros_like(state_s)

    h_s[...] = _rmsnorm(x_ref[0], nw_ref[...]).astype(BF16)

    lbp = lbp_ref[...]
    lbe = jnp.exp(lbp - jnp.max(lbp, axis=0, keepdims=True))
    lb_all = jnp.sum(lbe[0:layer + 1], axis=0, keepdims=True) / jnp.sum(lbe, axis=0, keepdims=True)

    tri = _chunk_tri(CUMSUM_BLOCK, HG_CHUNK, True)
    tri2 = jnp.concatenate([tri, tri], axis=1)
    r = lax.broadcasted_iota(jnp.int32, (sub, sub), 0)
    c = lax.broadcasted_iota(jnp.int32, (sub, sub), 1)
    causal = ((r // HG_CHUNK) == (c // HG_CHUNK)) & (c <= r)
    zeros = jnp.zeros((HG_CHUNK, HG_DK), BF16)
    subtiles = [slice(r0, r0 + sub) for r0 in range(0, ts, sub)]
    row_blocks = [slice(r0, r0 + CUMSUM_BLOCK) for r0 in range(0, ts, CUMSUM_BLOCK)]
    chunk_rows = [slice(ci * HG_CHUNK, (ci + 1) * HG_CHUNK) for ci in range(n_chunk)]

    def block_stages(c0):
        cb = slice(c0, c0 + width)
        wcol = lambda section: slice(section * HG_WIDTH + c0, section * HG_WIDTH + c0 + width)
        heads = [slice(c0 + i * HG_DK, c0 + (i + 1) * HG_DK) for i in range(width // HG_DK)]
        pf = _dot(h_s[...], w_ref[:, wcol(1)])
        pq = _dot(h_s[...], w_ref[:, wcol(0)])
        yield
        lb = lb_all[:, cb]
        f = lb + (1.0 - lb) * _sigmoid(pf)
        k = 1.0 - f
        hi, lo = _split_bf16(jnp.log(f))
        yield
        b = jnp.concatenate([_dot(tri2, jnp.concatenate([hi[rb], lo[rb]], axis=0)) for rb in row_blocks], axis=0)
        v_s[:, cb] = _dot(h_s[...], w_ref[:, wcol(2)]).astype(BF16)
        og = _dot(h_s[...], w_ref[:, wcol(3)])
        yield
        b3 = b.reshape(n_chunk, HG_CHUNK, width)
        b_ref = b3[:, HG_CHUNK // 2:HG_CHUNK // 2 + 1]
        b_last = b3[:, HG_CHUNK - 1:HG_CHUNK]
        k3 = k.reshape(n_chunk, HG_CHUNK, width)
        kr_s[:, cb] = (k3 * jnp.exp(b_ref - b3)).reshape(ts, width).astype(BF16)
        ki_s[:, cb] = (k3 * jnp.exp(b_last - b3)).reshape(ts, width).astype(BF16)
        dch_s[:, cb] = jnp.exp(b_last).reshape(n_chunk, width)
        q3 = _silu(pq).reshape(n_chunk, HG_CHUNK, width)
        qr_s[:, cb] = (q3 * jnp.exp(b3 - b_ref)).reshape(ts, width).astype(BF16)
        qi_s[:, cb] = (q3 * jnp.exp(b3)).reshape(ts, width).astype(BF16)
        o_s[:, cb] = _silu(og) * gnw_ref[:, cb]
        yield
        scores, incs = [], []
        for hs in heads:
            scores.append([jnp.where(causal, _dot_nt(qr_s[rs, hs], kr_s[rs, hs]), 0.0).astype(BF16)
                           for rs in subtiles])
            inc_h = []
            for pi in range(n_chunk // 2):
                ra, rb = chunk_rows[2 * pi], chunk_rows[2 * pi + 1]
                kib = jnp.concatenate([jnp.concatenate([ki_s[ra, hs], zeros], axis=1),
                                       jnp.concatenate([zeros, ki_s[rb, hs]], axis=1)], axis=0)
                inc = _dot_tn(v_s[ra.start:rb.stop, hs], kib)
                inc_h += [inc[:, :HG_DK], inc[:, HG_DK:]]
            incs.append(inc_h)
        yield
        o_intra, states = [], []
        for hi_, hs in enumerate(heads):
            o_intra.append([_dot(a, v_s[rs, hs]) for a, rs in zip(scores[hi_], subtiles)])
            z = hs.start // HG_DK
            st = state_s[z]
            st_h = []
            for ci in range(n_chunk):
                st_h.append(st.astype(BF16))
                st = dch_s[ci:ci + 1, hs] * st + incs[hi_][ci]
            state_s[z] = st
            states.append(st_h)
        yield
        for hi_, hs in enumerate(heads):
            o_inter = [_dot_nt(qi_s[rs, hs], states[hi_][ci]) for ci, rs in enumerate(chunk_rows)]
            o = jnp.concatenate(o_intra[hi_], axis=0) + jnp.concatenate(o_inter, axis=0)
            y_s[:, hs] = (o * lax.rsqrt(jnp.mean(o * o, axis=-1, keepdims=True) + EPS) * o_s[:, hs]).astype(BF16)

    def gate_stages():
        for c0 in range(0, D_MODEL, width):
            g_s[:, c0:c0 + width] = _sigmoid(_dot(h_s[...], wg_ref[:, c0:c0 + width]))
            yield

    _run_skewed([block_stages(c0) for c0 in range(0, HG_WIDTH, width)] + [gate_stages()])
    out_ref[0] = g_s[...] * _dot(y_s[...], wbr_ref[...])


def _hgrn2_branch(layer, x, norm_w, w_hg, w_gate, lb_param, gn_w, w_br):
    bsz, s, _ = x.shape
    ts = min(HG_SEQ_TILE, s)
    tile = lambda: pl.BlockSpec((1, ts, D_MODEL), lambda b, i: (b, i, 0))
    return pl.pallas_call(
        functools.partial(_hgrn2_kernel, layer),
        grid=(bsz, s // ts),
        in_specs=[tile(),
                  _const_spec((1, D_MODEL)),
                  _const_spec((D_MODEL, 4 * HG_WIDTH)),
                  _const_spec((D_MODEL, D_MODEL)),
                  _const_spec(lb_param.shape),
                  _const_spec((1, HG_WIDTH)),
                  _const_spec((HG_WIDTH, D_MODEL))],
        out_specs=tile(),
        out_shape=jax.ShapeDtypeStruct(x.shape, F32),
        scratch_shapes=[pltpu.VMEM((ts, HG_WIDTH), BF16)] * 7
                       + [pltpu.VMEM((ts, HG_WIDTH), F32)] * 2
                       + [pltpu.VMEM((ts // HG_CHUNK, HG_WIDTH), F32),
                          pltpu.VMEM((HG_HEADS, HG_DV, HG_DK), F32)],
        compiler_params=_params(2),
        name="hgrn2_branch",
    )(x, norm_w, w_hg, w_gate, lb_param, gn_w, w_br)


def _expand_heads(v, expand2):
    hi, lo = _split_bf16(v)
    return _dot(jnp.concatenate([hi, lo], axis=1), expand2)


def _ssd_kernel(x_ref, nw_ref, wz_ref, wxbc_ref, wdt_ref, wdtT_ref, wg_ref, convw_ref, convb_ref,
                dtb_ref, dtbT_ref, a_ref, aT_ref, dskip_ref, expand2_ref, normw_ref, wbr_ref, out_ref,
                raw_s, xs_s, xd_s, bm_s, cm_s, y_s, din_s, state_s):
    ts = x_ref.shape[1]
    lc = SSD_CHUNK
    n_chunk = ts // lc
    pad = 8
    n_lane_tiles = SSM_CONV_DIM // LANES

    @pl.when(pl.program_id(1) == 0)
    def _():
        state_s[...] = jnp.zeros_like(state_s)
        raw_s[:, 0:pad, :] = jnp.zeros((n_lane_tiles, pad, LANES), F32)

    h = _rmsnorm(x_ref[0], nw_ref[...]).astype(BF16)

    raw = _dot(h, wxbc_ref[...])
    cols = []
    for ct in range(n_lane_tiles):
        ls = slice(ct * LANES, (ct + 1) * LANES)
        raw_s[ct, pad:pad + ts, :] = raw[:, ls]
        acc = convb_ref[:, ls] + convw_ref[SSM_CONV - 1:SSM_CONV, ls] * raw[:, ls]
        for j in range(SSM_CONV - 1):
            off = pad - (SSM_CONV - 1) + j
            acc = acc + convw_ref[j:j + 1, ls] * raw_s[ct, off:off + ts, :]
        raw_s[ct, 0:pad, :] = raw_s[ct, ts:ts + pad, :]
        cols.append(acc)
    xbc = _silu(jnp.concatenate(cols, axis=1))
    xs = xbc[:, :SSM_INNER]
    bm_s[...] = xbc[:, SSM_INNER:SSM_INNER + SSM_BC].astype(BF16)
    cm_s[...] = xbc[:, SSM_INNER + SSM_BC:].astype(BF16)

    a = -jnp.exp(a_ref[...])
    dt = _softplus(_dot(h, wdt_ref[...]) + dtb_ref[...])
    cum = _chunk_cumsum_rows(dt * a, lc)
    aT = -jnp.exp(aT_ref[...])
    dtT = _softplus(_dot_nt(wdtT_ref[...], h) + dtbT_ref[...])
    cumT = _chunk_cumsum_lanes(dtT * aT, lc)

    cum3 = cum.reshape(n_chunk, lc, HEAD_PAD)
    cum_last = cum3[:, lc - 1:lc]
    dout = jnp.exp(cum_last - cum3).reshape(ts, HEAD_PAD)
    expand2 = expand2_ref[...]
    din_s[...] = _expand_heads(jnp.exp(cum), expand2)
    xd_s[...] = (xs * _expand_heads(dt * dout, expand2)).astype(BF16)
    y_s[...] = dskip_ref[...] * xs
    xs_s[...] = xs.astype(BF16)

    r = lax.broadcasted_iota(jnp.int32, (lc, lc), 0)
    c = lax.broadcasted_iota(jnp.int32, (lc, lc), 1)
    causal = c <= r
    head_in_group = lax.broadcasted_iota(jnp.int32, (1, SSM_GROUP_W), 1) // SSM_HEADDIM

    for gi in range(SSM_GROUPS):
        ns = slice(gi * SSM_STATE, (gi + 1) * SSM_STATE)
        ws = slice(gi * SSM_GROUP_W, (gi + 1) * SSM_GROUP_W)
        chunks = [slice(ci * lc, (ci + 1) * lc) for ci in range(n_chunk)]
        incs = [_dot_tn(bm_s[rs, ns], xd_s[rs, ws]) for rs in chunks]
        st = state_s[gi]
        states = []
        for ci in range(n_chunk):
            states.append(st.astype(BF16))
            dchunk = din_s[ci * lc + lc - 1:ci * lc + lc, ws]
            st = dchunk * st + incs[ci]
        state_s[gi] = st
        for ci, rs in enumerate(chunks):
            cmg = cm_s[rs, ns]
            cb = _dot_nt(cmg, bm_s[rs, ns])
            y = _dot(cmg, states[ci]) * din_s[rs, ws]
            xg = xs_s[rs, ws]
            for j in range(SSM_HPG):
                hd = gi * SSM_HPG + j
                seg = cum[rs, hd:hd + 1] - cumT[hd:hd + 1, rs]
                m = jnp.where(causal, cb * jnp.exp(seg) * dtT[hd:hd + 1, rs], 0.0).astype(BF16)
                y = y + _dot(m, jnp.where(head_in_group == j, xg, jnp.zeros_like(xg)))
            y_s[rs, ws] += y

    y = y_s[...] * _silu(_dot(h, wz_ref[...]))
    normw = normw_ref[...]
    for gi in range(SSM_GROUPS):
        ws = slice(gi * SSM_GROUP_W, (gi + 1) * SSM_GROUP_W)
        yg = y[:, ws]
        y_s[:, ws] = yg * lax.rsqrt(jnp.mean(yg * yg, axis=-1, keepdims=True) + EPS) * normw[:, ws]
    gate = _sigmoid(_dot(h, wg_ref[...]))
    out_ref[0] = gate * _dot(y_s[...].astype(BF16), wbr_ref[...])


def _ssd_branch(x, norm_w, w_z, w_xbc, w_dt, w_dtT, w_gate, conv_w, conv_b, dt_bias, dt_biasT,
                a_log, a_logT, d_skip_x, expand2, ssm_norm_w, w_br):
    bsz, s, _ = x.shape
    ts = min(SSD_SEQ_TILE, s)
    assert ts % SSD_CHUNK == 0 and SSD_CHUNK == CUMSUM_BLOCK
    tile = lambda: pl.BlockSpec((1, ts, D_MODEL), lambda b, i: (b, i, 0))
    consts = [norm_w, w_z, w_xbc, w_dt, w_dtT, w_gate, conv_w, conv_b, dt_bias, dt_biasT,
              a_log, a_logT, d_skip_x, expand2, ssm_norm_w, w_br]
    return pl.pallas_call(
        _ssd_kernel,
        grid=(bsz, s // ts),
        in_specs=[tile()] + [_const_spec(c.shape) for c in consts],
        out_specs=tile(),
        out_shape=jax.ShapeDtypeStruct(x.shape, F32),
        scratch_shapes=[pltpu.VMEM((SSM_CONV_DIM // LANES, ts + 8, LANES), F32),
                        pltpu.VMEM((ts, SSM_INNER), BF16),
                        pltpu.VMEM((ts, SSM_INNER), BF16),
                        pltpu.VMEM((ts, SSM_BC), BF16),
                        pltpu.VMEM((ts, SSM_BC), BF16),
                        pltpu.VMEM((ts, SSM_INNER), F32),
                        pltpu.VMEM((ts, SSM_INNER), F32),
                        pltpu.VMEM((SSM_GROUPS, SSM_STATE, SSM_GROUP_W), F32)],
        compiler_params=_params(2),
        name="ssd_branch",
    )(x, *consts)


def _xattn_kernel(x_ref, mhg_ref, mssm_ref, k_ref, v_ref, nw_ref, wq_ref, wg_ref, wbr_ref,
                  wout_ref, npost_ref, out_ref, o_s):
    x = x_ref[0]
    h = _rmsnorm(x, nw_ref[...]).astype(BF16)
    q = _dot(h, wq_ref[...]).astype(BF16)
    for z in range(XA_HEADS):
        hs = slice(z * XA_HEADDIM, (z + 1) * XA_HEADDIM)
        sc = _dot_nt(q[:, hs], k_ref[0, :, hs]) * (XA_HEADDIM ** -0.5)
        e = jnp.exp(sc - jnp.max(sc, axis=-1, keepdims=True))
        p = e / jnp.sum(e, axis=-1, keepdims=True)
        o_s[:, hs] = _dot(p.astype(BF16), v_ref[0, :, hs]).astype(BF16)
    gate = _sigmoid(_dot(h, wg_ref[...]))
    merged = mhg_ref[0] + mssm_ref[0] + gate * _dot(o_s[...], wbr_ref[...])
    out_ref[0] = x + _rmsnorm(_dot(merged.astype(BF16), wout_ref[...]), npost_ref[...])


def _xattn_merge(x, m_hg, m_ssm, k, v, norm_w, w_q, w_gate, w_br, w_out, norm_post):
    bsz, s, _ = x.shape
    ts = min(XA_SEQ_TILE, s)
    tile = lambda: pl.BlockSpec((1, ts, D_MODEL), lambda b, i: (b, i, 0))
    kv = lambda: pl.BlockSpec((1, N_MEM, XA_WIDTH), lambda b, i: (b, 0, 0))
    sq = lambda: _const_spec((D_MODEL, D_MODEL))
    return pl.pallas_call(
        _xattn_kernel,
        grid=(bsz, s // ts),
        in_specs=[tile(), tile(), tile(), kv(), kv(), _const_spec((1, D_MODEL)),
                  sq(), sq(), sq(), sq(), _const_spec((1, D_MODEL))],
        out_specs=tile(),
        out_shape=jax.ShapeDtypeStruct(x.shape, F32),
        scratch_shapes=[pltpu.VMEM((ts, XA_WIDTH), BF16)],
        compiler_params=_params(2),
        name="xattn_merge",
    )(x, m_hg, m_ssm, k, v, norm_w, w_q, w_gate, w_br, w_out, norm_post)


def _ffn_kernel(x_ref, npre_ref, wup_ref, wdown_ref, npost_ref, out_ref):
    x = x_ref[...]
    h = _rmsnorm(x, npre_ref[...]).astype(BF16)
    g = _dot(h, wup_ref[:, :FFN_HIDDEN])
    u = _dot(h, wup_ref[:, FFN_HIDDEN:])
    act = (_silu(g) * u).astype(BF16)
    out_ref[...] = x + _rmsnorm(_dot(act, wdown_ref[...]), npost_ref[...])


def _ffn(x2d, norm_pre, w_up, w_down, norm_post):
    rows = x2d.shape[0]
    tile = min(ROW_TILE, rows)
    spec = lambda: pl.BlockSpec((tile, D_MODEL), lambda i: (i, 0))
    return pl.pallas_call(
        _ffn_kernel,
        grid=(rows // tile,),
        in_specs=[spec(), _const_spec((1, D_MODEL)), _const_spec((D_MODEL, 2 * FFN_HIDDEN)),
                  _const_spec((FFN_HIDDEN, D_MODEL)), _const_spec((1, D_MODEL))],
        out_specs=spec(),
        out_shape=jax.ShapeDtypeStruct(x2d.shape, F32),
        compiler_params=_params(1),
        name="swiglu_ffn",
    )(x2d, norm_pre, w_up, w_down, norm_post)


def _pad_heads(v, fill=0.0):
    row = jnp.full((1, HEAD_PAD), fill, F32).at[0, :SSM_HEADS].set(v.astype(F32))
    return row, row.reshape(HEAD_PAD, 1)


def kernel(x, mem, norm_mix_pre, norm_mix_post, norm_ffn_pre, norm_ffn_post, norm_mem, w_in,
           hg_lb_param, hg_norm_w, conv_w, conv_b, dt_bias, a_log, d_skip, ssm_norm_w, w_mem_kv,
           w_br_hg, w_br_ssm, w_br_xa, w_out, w_ffn_up, w_ffn_down):
    bsz, s, d = x.shape
    depth = w_in.shape[0]
    assert d == D_MODEL and all(s % min(t, s) == 0 for t in (HG_SEQ_TILE, SSD_SEQ_TILE, XA_SEQ_TILE))
    row = lambda v: v.reshape(1, -1).astype(F32)

    expand = (jnp.arange(HEAD_PAD)[:, None] == (jnp.arange(SSM_INNER)[None, :] // SSM_HEADDIM)).astype(BF16)
    expand2 = jnp.concatenate([expand, expand], axis=0)

    o_hg = 4 * HG_WIDTH
    o_z, o_xbc = o_hg, o_hg + SSM_INNER
    o_dt = o_xbc + SSM_CONV_DIM
    o_q = o_dt + SSM_HEADS
    o_gate = o_q + XA_WIDTH

    for l in range(depth):
        w = w_in[l]
        w_hg = w[:, :o_hg].astype(BF16)
        w_z = w[:, o_z:o_xbc].astype(BF16)
        w_xbc = w[:, o_xbc:o_dt].astype(BF16)
        w_dt = jnp.zeros((D_MODEL, HEAD_PAD), BF16).at[:, :SSM_HEADS].set(w[:, o_dt:o_q].astype(BF16))
        w_q = w[:, o_q:o_gate].astype(BF16)
        w_g = [w[:, o_gate + i * D_MODEL:o_gate + (i + 1) * D_MODEL].astype(BF16) for i in range(3)]
        dtb, dtbT = _pad_heads(dt_bias[l])
        alog, alogT = _pad_heads(a_log[l])
        d_skip_x = jnp.repeat(d_skip[l].astype(F32), SSM_HEADDIM).reshape(1, SSM_INNER)
        n_pre = row(norm_mix_pre[l])

        k, v = _mem_kv(mem.reshape(-1, D_MODEL), row(norm_mem[l]), w_mem_kv[l].astype(BF16))
        k = k.reshape(bsz, -1, XA_WIDTH)
        v = v.reshape(bsz, -1, XA_WIDTH)
        m_hg = _hgrn2_branch(l, x, n_pre, w_hg, w_g[0], hg_lb_param.astype(F32),
                             row(hg_norm_w[l]), w_br_hg[l].astype(BF16))
        m_ssm = _ssd_branch(x, n_pre, w_z, w_xbc, w_dt, w_dt.T, w_g[1], conv_w[l].astype(F32),
                            row(conv_b[l]), dtb, dtbT, alog, alogT, d_skip_x, expand2,
                            row(ssm_norm_w[l]), w_br_ssm[l].astype(BF16))
        x = _xattn_merge(x, m_hg, m_ssm, k, v, n_pre, w_q, w_g[2], w_br_xa[l].astype(BF16),
                         w_out[l].astype(BF16), row(norm_mix_post[l]))
        x = _ffn(x.reshape(-1, D_MODEL), row(norm_ffn_pre[l]), w_ffn_up[l].astype(BF16),
                 w_ffn_down[l].astype(BF16), row(norm_ffn_post[l])).reshape(bsz, s, d)
    return x
```

```python
import functools

import jax
import jax.numpy as jnp
from jax import lax
from jax.experimental import pallas as pl
from jax.experimental.pallas import tpu as pltpu

F32 = jnp.float32
BF16 = jnp.bfloat16

EPS = 1e-6
D_MODEL = 1024
N_MEM = 256

HG_HEADS = 8
HG_DK = 128
HG_DV = 128
HG_WIDTH = HG_HEADS * HG_DV
HG_CHUNK = 32

SSM_INNER = 2048
SSM_HEADDIM = 64
SSM_HEADS = 32
SSM_STATE = 128
SSM_GROUPS = 8
SSM_HPG = 4
SSM_CONV = 4
SSM_BC = SSM_GROUPS * SSM_STATE
SSM_CONV_DIM = SSM_INNER + 2 * SSM_BC
SSM_GROUP_W = SSM_HPG * SSM_HEADDIM
LANES = 128
MXU_DEPTH = 256
HEAD_PAD = LANES

XA_HEADS = 4
XA_HEADDIM = 256
XA_WIDTH = 1024

FFN_HIDDEN = 2816

VMEM_LIMIT_BYTES = 56 * 1024 * 1024

HG_SEQ_TILE = 512
SSD_SEQ_TILE = 512
XA_SEQ_TILE = 512
ROW_TILE = 512
HG_SUBTILE = 256
HG_COL_BLOCK = 256
SSD_CHUNK = 128
CUMSUM_BLOCK = 128


def _dot(a, b):
    return jnp.dot(a, b, preferred_element_type=F32)


def _dot_nt(a, b):
    return lax.dot_general(a, b, (((1,), (1,)), ((), ())), preferred_element_type=F32)


def _dot_tn(a, b):
    return lax.dot_general(a, b, (((0,), (0,)), ((), ())), preferred_element_type=F32)


def _split_bf16(v):
    hi = v.astype(BF16)
    return hi, (v - hi.astype(F32)).astype(BF16)


def _rmsnorm(x, w):
    return x * lax.rsqrt(jnp.mean(x * x, axis=-1, keepdims=True) + EPS) * w


def _sigmoid(x):
    return 1.0 / (1.0 + jnp.exp(-x))


def _silu(x):
    return x * _sigmoid(x)


def _softplus(x):
    return jnp.maximum(x, 0.0) + jnp.log1p(jnp.exp(-jnp.abs(x)))


def _chunk_tri(n, chunk, lower):
    r = lax.broadcasted_iota(jnp.int32, (n, n), 0)
    c = lax.broadcasted_iota(jnp.int32, (n, n), 1)
    same = (r // chunk) == (c // chunk)
    tri = (c <= r) if lower else (r <= c)
    return jnp.where(same & tri, 1.0, 0.0).astype(BF16)


def _chunk_cumsum_rows(v, chunk):
    blk = max(CUMSUM_BLOCK, chunk)
    tri = _chunk_tri(blk, chunk, True)
    hi, lo = _split_bf16(v)
    if 2 * blk > MXU_DEPTH:
        return jnp.concatenate(
            [_dot(tri, hi[r0:r0 + blk]) + _dot(tri, lo[r0:r0 + blk]) for r0 in range(0, v.shape[0], blk)], axis=0)
    tri2 = jnp.concatenate([tri, tri], axis=1)
    return jnp.concatenate(
        [_dot(tri2, jnp.concatenate([hi[r0:r0 + blk], lo[r0:r0 + blk]], axis=0))
         for r0 in range(0, v.shape[0], blk)], axis=0)


def _chunk_cumsum_lanes(v, chunk):
    blk = max(CUMSUM_BLOCK, chunk)
    tri = _chunk_tri(blk, chunk, False)
    hi, lo = _split_bf16(v)
    if 2 * blk > MXU_DEPTH:
        return jnp.concatenate(
            [_dot(hi[:, c0:c0 + blk], tri) + _dot(lo[:, c0:c0 + blk], tri) for c0 in range(0, v.shape[1], blk)], axis=1)
    tri2 = jnp.concatenate([tri, tri], axis=0)
    return jnp.concatenate(
        [_dot(jnp.concatenate([hi[:, c0:c0 + blk], lo[:, c0:c0 + blk]], axis=1), tri2)
         for c0 in range(0, v.shape[1], blk)], axis=1)


def _const_spec(shape):
    nd = len(shape)
    return pl.BlockSpec(shape, lambda *_: (0,) * nd, pipeline_mode=pl.Buffered(1))


def _params(n_grid):
    return pltpu.CompilerParams(dimension_semantics=("arbitrary",) * n_grid,
                                vmem_limit_bytes=VMEM_LIMIT_BYTES)


def _mem_kv_kernel(mem_ref, nw_ref, w_ref, k_ref, v_ref):
    mn = _rmsnorm(mem_ref[...], nw_ref[...]).astype(BF16)
    kv = _dot(mn, w_ref[...])
    k_ref[...] = kv[:, :XA_WIDTH].astype(BF16)
    v_ref[...] = kv[:, XA_WIDTH:].astype(BF16)


def _mem_kv(mem2d, norm_w, w_kv):
    rows = mem2d.shape[0]
    tile = min(ROW_TILE, rows)
    out = jax.ShapeDtypeStruct((rows, XA_WIDTH), BF16)
    return pl.pallas_call(
        _mem_kv_kernel,
        grid=(rows // tile,),
        in_specs=[pl.BlockSpec((tile, D_MODEL), lambda i: (i, 0)),
                  _const_spec((1, D_MODEL)),
                  _const_spec((D_MODEL, 2 * XA_WIDTH))],
        out_specs=[pl.BlockSpec((tile, XA_WIDTH), lambda i: (i, 0))] * 2,
        out_shape=[out, out],
        compiler_params=_params(1),
        name="mem_kv",
    )(mem2d, norm_w, w_kv)


_DONE = object()


def _run_skewed(stage_generators):
    waiting = list(stage_generators)
    active = []
    while waiting or active:
        if waiting:
            active.append(waiting.pop(0))
        active = [g for g in active if next(g, _DONE) is not _DONE]


def _hgrn2_kernel(layer, x_ref, nw_ref, w_ref, wg_ref, lbp_ref, gnw_ref, wbr_ref, out_ref,
                  qr_s, kr_s, qi_s, ki_s, v_s, y_s, h_s, o_s, g_s, dch_s, state_s):
    ts = x_ref.shape[1]
    n_chunk = ts // HG_CHUNK
    sub = min(HG_SUBTILE, ts)
    width = HG_COL_BLOCK
    assert n_chunk % 2 == 0 and ts % sub == 0 and ts % CUMSUM_BLOCK == 0 and width % HG_DK == 0

    @pl.when(pl.program_id(1) == 0)
    def _():
        state_s[...] = jnp.zeros_like(state_s)

    h_s[...] = _rmsnorm(x_ref[0], nw_ref[...]).astype(BF16)

    lbp = lbp_ref[...]
    lbe = jnp.exp(lbp - jnp.max(lbp, axis=0, keepdims=True))
    lb_all = jnp.sum(lbe[0:layer + 1], axis=0, keepdims=True) / jnp.sum(lbe, axis=0, keepdims=True)

    tri = _chunk_tri(CUMSUM_BLOCK, HG_CHUNK, True)
    tri2 = jnp.concatenate([tri, tri], axis=1)
    r = lax.broadcasted_iota(jnp.int32, (sub, sub), 0)
    c = lax.broadcasted_iota(jnp.int32, (sub, sub), 1)
    causal = ((r // HG_CHUNK) == (c // HG_CHUNK)) & (c <= r)
    zeros = jnp.zeros((HG_CHUNK, HG_DK), BF16)
    subtiles = [slice(r0, r0 + sub) for r0 in range(0, ts, sub)]
    row_blocks = [slice(r0, r0 + CUMSUM_BLOCK) for r0 in range(0, ts, CUMSUM_BLOCK)]
    chunk_rows = [slice(ci * HG_CHUNK, (ci + 1) * HG_CHUNK) for ci in range(n_chunk)]

    def block_stages(c0):
        cb = slice(c0, c0 + width)
        wcol = lambda section: slice(section * HG_WIDTH + c0, section * HG_WIDTH + c0 + width)
        heads = [slice(c0 + i * HG_DK, c0 + (i + 1) * HG_DK) for i in range(width // HG_DK)]
        pf = _dot(h_s[...], w_ref[:, wcol(1)])
        pq = _dot(h_s[...], w_ref[:, wcol(0)])
        yield
        lb = lb_all[:, cb]
        f = lb + (1.0 - lb) * _sigmoid(pf)
        k = 1.0 - f
        hi, lo = _split_bf16(jnp.log(f))
        yield
        b = jnp.concatenate([_dot(tri2, jnp.concatenate([hi[rb], lo[rb]], axis=0)) for rb in row_blocks], axis=0)
        v_s[:, cb] = _dot(h_s[...], w_ref[:, wcol(2)]).astype(BF16)
        og = _dot(h_s[...], w_ref[:, wcol(3)])
        yield
        b3 = b.reshape(n_chunk, HG_CHUNK, width)
        b_ref = b3[:, HG_CHUNK // 2:HG_CHUNK // 2 + 1]
        b_last = b3[:, HG_CHUNK - 1:HG_CHUNK]
        k3 = k.reshape(n_chunk, HG_CHUNK, width)
        kr_s[:, cb] = (k3 * jnp.exp(b_ref - b3)).reshape(ts, width).astype(BF16)
        ki_s[:, cb] = (k3 * jnp.exp(b_last - b3)).reshape(ts, width).astype(BF16)
        dch_s[:, cb] = jnp.exp(b_last).reshape(n_chunk, width)
        q3 = _silu(pq).reshape(n_chunk, HG_CHUNK, width)
        qr_s[:, cb] = (q3 * jnp.exp(b3 - b_ref)).reshape(ts, width).astype(BF16)
        qi_s[:, cb] = (q3 * jnp.exp(b3)).reshape(ts, width).astype(BF16)
        o_s[:, cb] = _silu(og) * gnw_ref[:, cb]
        yield
        scores, incs = [], []
        for hs in heads:
            scores.append([jnp.where(causal, _dot_nt(qr_s[rs, hs], kr_s[rs, hs]), 0.0).astype(BF16)
                           for rs in subtiles])
            inc_h = []
            for pi in range(n_chunk // 2):
                ra, rb = chunk_rows[2 * pi], chunk_rows[2 * pi + 1]
                kib = jnp.concatenate([jnp.concatenate([ki_s[ra, hs], zeros], axis=1),
                                       jnp.concatenate([zeros, ki_s[rb, hs]], axis=1)], axis=0)
                inc = _dot_tn(v_s[ra.start:rb.stop, hs], kib)
                inc_h += [inc[:, :HG_DK], inc[:, HG_DK:]]
            incs.append(inc_h)
        yield
        o_intra, states = [], []
        for hi_, hs in enumerate(heads):
            o_intra.append([_dot(a, v_s[rs, hs]) for a, rs in zip(scores[hi_], subtiles)])
            z = hs.start // HG_DK
            st = state_s[z]
            st_h = []
            for ci in range(n_chunk):
                st_h.append(st.astype(BF16))
                st = dch_s[ci:ci + 1, hs] * st + incs[hi_][ci]
            state_s[z] = st
            states.append(st_h)
        yield
        for hi_, hs in enumerate(heads):
            o_inter = [_dot_nt(qi_s[rs, hs], states[hi_][ci]) for ci, rs in enumerate(chunk_rows)]
            o = jnp.concatenate(o_intra[hi_], axis=0) + jnp.concatenate(o_inter, axis=0)
            y_s[:, hs] = (o * lax.rsqrt(jnp.mean(o * o, axis=-1, keepdims=True) + EPS) * o_s[:, hs]).astype(BF16)

    def gate_stages():
        for c0 in range(0, D_MODEL, width):
            g_s[:, c0:c0 + width] = _sigmoid(_dot(h_s[...], wg_ref[:, c0:c0 + width]))
            yield

    _run_skewed([block_stages(c0) for c0 in range(0, HG_WIDTH, width)] + [gate_stages()])
    out_ref[0] = g_s[...] * _dot(y_s[...], wbr_ref[...])


def _hgrn2_branch(layer, x, norm_w, w_hg, w_gate, lb_param, gn_w, w_br):
    bsz, s, _ = x.shape
    ts = min(HG_SEQ_TILE, s)
    tile = lambda: pl.BlockSpec((1, ts, D_MODEL), lambda b, i: (b, i, 0))
    return pl.pallas_call(
        functools.partial(_hgrn2_kernel, layer),
        grid=(bsz, s // ts),
        in_specs=[tile(),
                  _const_spec((1, D_MODEL)),
                  _const_spec((D_MODEL, 4 * HG_WIDTH)),
                  _const_spec((D_MODEL, D_MODEL)),
                  _const_spec(lb_param.shape),
                  _const_spec((1, HG_WIDTH)),
                  _const_spec((HG_WIDTH, D_MODEL))],
        out_specs=tile(),
        out_shape=jax.ShapeDtypeStruct(x.shape, F32),
        scratch_shapes=[pltpu.VMEM((ts, HG_WIDTH), BF16)] * 7
                       + [pltpu.VMEM((ts, HG_WIDTH), F32)] * 2
                       + [pltpu.VMEM((ts // HG_CHUNK, HG_WIDTH), F32),
                          pltpu.VMEM((HG_HEADS, HG_DV, HG_DK), F32)],
        compiler_params=_params(2),
        name="hgrn2_branch",
    )(x, norm_w, w_hg, w_gate, lb_param, gn_w, w_br)


def _expand_heads(v, expand2):
    hi, lo = _split_bf16(v)
    return _dot(jnp.concatenate([hi, lo], axis=1), expand2)


def _ssd_kernel(x_ref, nw_ref, wz_ref, wxbc_ref, wdt_ref, wdtT_ref, wg_ref, convw_ref, convb_ref,
                dtb_ref, dtbT_ref, a_ref, aT_ref, dskip_ref, expand2_ref, normw_ref, wbr_ref, out_ref,
                raw_s, h_s, yn_s, g_s, state_s):
    ts = x_ref.shape[1]
    lc = SSD_CHUNK
    n_chunk = ts // lc
    pad = 8
    gw = SSM_GROUP_W + 2 * SSM_STATE
    slabs_per_group = gw // LANES

    @pl.when(pl.program_id(1) == 0)
    def _():
        state_s[...] = jnp.zeros_like(state_s)
        raw_s[:, 0:pad, :] = jnp.zeros((SSM_CONV_DIM // LANES, pad, LANES), F32)

    h_s[...] = _rmsnorm(x_ref[0], nw_ref[...]).astype(BF16)

    a = -jnp.exp(a_ref[...])
    dt = _softplus(_dot(h_s[...], wdt_ref[...]) + dtb_ref[...])
    cum = _chunk_cumsum_rows(dt * a, lc)
    aT = -jnp.exp(aT_ref[...])
    dtT = _softplus(_dot_nt(wdtT_ref[...], h_s[...]) + dtbT_ref[...])
    cumT = _chunk_cumsum_lanes(dtT * aT, lc)
    cum3 = cum.reshape(n_chunk, lc, HEAD_PAD)
    dout = jnp.exp(cum3[:, lc - 1:lc] - cum3).reshape(ts, HEAD_PAD)
    din_hl = jnp.concatenate(_split_bf16(jnp.exp(cum)), axis=1)
    dtd_hl = jnp.concatenate(_split_bf16(dt * dout), axis=1)

    r = lax.broadcasted_iota(jnp.int32, (lc, lc), 0)
    c = lax.broadcasted_iota(jnp.int32, (lc, lc), 1)
    causal = c <= r
    head_in_group = lax.broadcasted_iota(jnp.int32, (1, SSM_GROUP_W), 1) // SSM_HEADDIM
    chunks = [slice(ci * lc, (ci + 1) * lc) for ci in range(n_chunk)]

    def group_stages(gi):
        ws = slice(gi * SSM_GROUP_W, (gi + 1) * SSM_GROUP_W)
        pcols = slice(gi * gw, (gi + 1) * gw)
        raw = _dot(h_s[...], wxbc_ref[:, pcols])
        din = _dot(din_hl, expand2_ref[:, ws])
        dtd = _dot(dtd_hl, expand2_ref[:, ws])
        yield
        cols = []
        for si in range(slabs_per_group):
            ct = gi * slabs_per_group + si
            ls = slice(si * LANES, (si + 1) * LANES)
            pls = slice(gi * gw + si * LANES, gi * gw + (si + 1) * LANES)
            raw_s[ct, pad:pad + ts, :] = raw[:, ls]
            acc = convb_ref[:, pls] + convw_ref[SSM_CONV - 1:SSM_CONV, pls] * raw[:, ls]
            for j in range(SSM_CONV - 1):
                off = pad - (SSM_CONV - 1) + j
                acc = acc + convw_ref[j:j + 1, pls] * raw_s[ct, off:off + ts, :]
            raw_s[ct, 0:pad, :] = raw_s[ct, ts:ts + pad, :]
            cols.append(_silu(acc))
        xs = jnp.concatenate(cols[:SSM_GROUP_W // LANES], axis=1)
        bm = cols[-2].astype(BF16)
        cm = cols[-1].astype(BF16)
        xd = (xs * dtd).astype(BF16)
        xs_bf = xs.astype(BF16)
        y0 = dskip_ref[:, ws] * xs
        yield
        incs = [_dot_tn(bm[rs], xd[rs]) for rs in chunks]
        cbs = [_dot_nt(cm[rs], bm[rs]) for rs in chunks]
        zp = _dot(h_s[...], wz_ref[:, ws])
        yield
        st = state_s[gi]
        states = []
        for ci in range(n_chunk):
            states.append(st.astype(BF16))
            st = din[ci * lc + lc - 1:ci * lc + lc] * st + incs[ci]
        state_s[gi] = st
        ms = []
        for ci, rs in enumerate(chunks):
            for j in range(SSM_HPG):
                hd = gi * SSM_HPG + j
                seg = cum[rs, hd:hd + 1] - cumT[hd:hd + 1, rs]
                ms.append(jnp.where(causal, cbs[ci] * jnp.exp(seg) * dtT[hd:hd + 1, rs], 0.0).astype(BF16))
        yield
        ys = []
        for ci, rs in enumerate(chunks):
            y = _dot(cm[rs], states[ci]) * din[rs]
            for j in range(SSM_HPG):
                xj = jnp.where(head_in_group == j, xs_bf[rs], jnp.zeros_like(xs_bf[rs]))
                y = y + _dot(ms[ci * SSM_HPG + j], xj)
            ys.append(y)
        yield
        y = (y0 + jnp.concatenate(ys, axis=0)) * _silu(zp)
        yn_s[:, ws] = (y * lax.rsqrt(jnp.mean(y * y, axis=-1, keepdims=True) + EPS) * normw_ref[:, ws]).astype(BF16)

    def gate_stages():
        for c0 in range(0, D_MODEL, SSM_GROUP_W):
            g_s[:, c0:c0 + SSM_GROUP_W] = _sigmoid(_dot(h_s[...], wg_ref[:, c0:c0 + SSM_GROUP_W]))
            yield

    _run_skewed([group_stages(gi) for gi in range(SSM_GROUPS)] + [gate_stages()])
    out_ref[0] = g_s[...] * _dot(yn_s[...], wbr_ref[...])


def _ssd_branch(x, norm_w, w_z, w_xbc, w_dt, w_dtT, w_gate, conv_w, conv_b, dt_bias, dt_biasT,
                a_log, a_logT, d_skip_x, expand2, ssm_norm_w, w_br):
    bsz, s, _ = x.shape
    ts = min(SSD_SEQ_TILE, s)
    assert ts % SSD_CHUNK == 0 and SSD_CHUNK % CUMSUM_BLOCK == 0
    tile = lambda: pl.BlockSpec((1, ts, D_MODEL), lambda b, i: (b, i, 0))
    consts = [norm_w, w_z, w_xbc, w_dt, w_dtT, w_gate, conv_w, conv_b, dt_bias, dt_biasT,
              a_log, a_logT, d_skip_x, expand2, ssm_norm_w, w_br]
    return pl.pallas_call(
        _ssd_kernel,
        grid=(bsz, s // ts),
        in_specs=[tile()] + [_const_spec(c.shape) for c in consts],
        out_specs=tile(),
        out_shape=jax.ShapeDtypeStruct(x.shape, F32),
        scratch_shapes=[pltpu.VMEM((SSM_CONV_DIM // LANES, ts + 8, LANES), F32),
                        pltpu.VMEM((ts, D_MODEL), BF16),
                        pltpu.VMEM((ts, SSM_INNER), BF16),
                        pltpu.VMEM((ts, D_MODEL), F32),
                        pltpu.VMEM((SSM_GROUPS, SSM_STATE, SSM_GROUP_W), F32)],
        compiler_params=_params(2),
        name="ssd_branch",
    )(x, *consts)


def _group_xbc_columns():
    cols = []
    for gi in range(SSM_GROUPS):
        cols += list(range(gi * SSM_GROUP_W, (gi + 1) * SSM_GROUP_W))
        cols += list(range(SSM_INNER + gi * SSM_STATE, SSM_INNER + (gi + 1) * SSM_STATE))
        cols += list(range(SSM_INNER + SSM_BC + gi * SSM_STATE, SSM_INNER + SSM_BC + (gi + 1) * SSM_STATE))
    return jnp.asarray(cols, dtype=jnp.int32)


def _xattn_kernel(x_ref, mhg_ref, mssm_ref, k_ref, v_ref, nw_ref, wq_ref, wg_ref, wbr_ref,
                  wout_ref, npost_ref, out_ref, o_s):
    x = x_ref[0]
    h = _rmsnorm(x, nw_ref[...]).astype(BF16)
    q = _dot(h, wq_ref[...]).astype(BF16)
    for z in range(XA_HEADS):
        hs = slice(z * XA_HEADDIM, (z + 1) * XA_HEADDIM)
        sc = _dot_nt(q[:, hs], k_ref[0, :, hs]) * (XA_HEADDIM ** -0.5)
        e = jnp.exp(sc - jnp.max(sc, axis=-1, keepdims=True))
        p = e / jnp.sum(e, axis=-1, keepdims=True)
        o_s[:, hs] = _dot(p.astype(BF16), v_ref[0, :, hs]).astype(BF16)
    gate = _sigmoid(_dot(h, wg_ref[...]))
    merged = mhg_ref[0] + mssm_ref[0] + gate * _dot(o_s[...], wbr_ref[...])
    out_ref[0] = x + _rmsnorm(_dot(merged.astype(BF16), wout_ref[...]), npost_ref[...])


def _xattn_merge(x, m_hg, m_ssm, k, v, norm_w, w_q, w_gate, w_br, w_out, norm_post):
    bsz, s, _ = x.shape
    ts = min(XA_SEQ_TILE, s)
    tile = lambda: pl.BlockSpec((1, ts, D_MODEL), lambda b, i: (b, i, 0))
    kv = lambda: pl.BlockSpec((1, N_MEM, XA_WIDTH), lambda b, i: (b, 0, 0))
    sq = lambda: _const_spec((D_MODEL, D_MODEL))
    return pl.pallas_call(
        _xattn_kernel,
        grid=(bsz, s // ts),
        in_specs=[tile(), tile(), tile(), kv(), kv(), _const_spec((1, D_MODEL)),
                  sq(), sq(), sq(), sq(), _const_spec((1, D_MODEL))],
        out_specs=tile(),
        out_shape=jax.ShapeDtypeStruct(x.shape, F32),
        scratch_shapes=[pltpu.VMEM((ts, XA_WIDTH), BF16)],
        compiler_params=_params(2),
        name="xattn_merge",
    )(x, m_hg, m_ssm, k, v, norm_w, w_q, w_gate, w_br, w_out, norm_post)


def _ffn_kernel(x_ref, npre_ref, wup_ref, wdown_ref, npost_ref, out_ref):
    x = x_ref[...]
    h = _rmsnorm(x, npre_ref[...]).astype(BF16)
    g = _dot(h, wup_ref[:, :FFN_HIDDEN])
    u = _dot(h, wup_ref[:, FFN_HIDDEN:])
    act = (_silu(g) * u).astype(BF16)
    out_ref[...] = x + _rmsnorm(_dot(act, wdown_ref[...]), npost_ref[...])


def _ffn(x2d, norm_pre, w_up, w_down, norm_post):
    rows = x2d.shape[0]
    tile = min(ROW_TILE, rows)
    spec = lambda: pl.BlockSpec((tile, D_MODEL), lambda i: (i, 0))
    return pl.pallas_call(
        _ffn_kernel,
        grid=(rows // tile,),
        in_specs=[spec(), _const_spec((1, D_MODEL)), _const_spec((D_MODEL, 2 * FFN_HIDDEN)),
                  _const_spec((FFN_HIDDEN, D_MODEL)), _const_spec((1, D_MODEL))],
        out_specs=spec(),
        out_shape=jax.ShapeDtypeStruct(x2d.shape, F32),
        compiler_params=_params(1),
        name="swiglu_ffn",
    )(x2d, norm_pre, w_up, w_down, norm_post)


def _pad_heads(v, fill=0.0):
    row = jnp.full((1, HEAD_PAD), fill, F32).at[0, :SSM_HEADS].set(v.astype(F32))
    return row, row.reshape(HEAD_PAD, 1)


def kernel(x, mem, norm_mix_pre, norm_mix_post, norm_ffn_pre, norm_ffn_post, norm_mem, w_in,
           hg_lb_param, hg_norm_w, conv_w, conv_b, dt_bias, a_log, d_skip, ssm_norm_w, w_mem_kv,
           w_br_hg, w_br_ssm, w_br_xa, w_out, w_ffn_up, w_ffn_down):
    bsz, s, d = x.shape
    depth = w_in.shape[0]
    assert d == D_MODEL and all(s % min(t, s) == 0 for t in (HG_SEQ_TILE, SSD_SEQ_TILE, XA_SEQ_TILE))
    row = lambda v: v.reshape(1, -1).astype(F32)

    expand = (jnp.arange(HEAD_PAD)[:, None] == (jnp.arange(SSM_INNER)[None, :] // SSM_HEADDIM)).astype(BF16)
    expand2 = jnp.concatenate([expand, expand], axis=0)

    xbc_cols = _group_xbc_columns()
    o_hg = 4 * HG_WIDTH
    o_z, o_xbc = o_hg, o_hg + SSM_INNER
    o_dt = o_xbc + SSM_CONV_DIM
    o_q = o_dt + SSM_HEADS
    o_gate = o_q + XA_WIDTH

    for l in range(depth):
        w = w_in[l]
        w_hg = w[:, :o_hg].astype(BF16)
        w_z = w[:, o_z:o_xbc].astype(BF16)
        w_xbc = w[:, o_xbc:o_dt][:, xbc_cols].astype(BF16)
        w_dt = jnp.zeros((D_MODEL, HEAD_PAD), BF16).at[:, :SSM_HEADS].set(w[:, o_dt:o_q].astype(BF16))
        w_q = w[:, o_q:o_gate].astype(BF16)
        w_g = [w[:, o_gate + i * D_MODEL:o_gate + (i + 1) * D_MODEL].astype(BF16) for i in range(3)]
        dtb, dtbT = _pad_heads(dt_bias[l])
        alog, alogT = _pad_heads(a_log[l])
        d_skip_x = jnp.repeat(d_skip[l].astype(F32), SSM_HEADDIM).reshape(1, SSM_INNER)
        n_pre = row(norm_mix_pre[l])

        k, v = _mem_kv(mem.reshape(-1, D_MODEL), row(norm_mem[l]), w_mem_kv[l].astype(BF16))
        k = k.reshape(bsz, -1, XA_WIDTH)
        v = v.reshape(bsz, -1, XA_WIDTH)
        m_hg = _hgrn2_branch(l, x, n_pre, w_hg, w_g[0], hg_lb_param.astype(F32),
                             row(hg_norm_w[l]), w_br_hg[l].astype(BF16))
        m_ssm = _ssd_branch(x, n_pre, w_z, w_xbc, w_dt, w_dt.T, w_g[1], conv_w[l].astype(F32)[:, xbc_cols],
                            row(conv_b[l])[:, xbc_cols], dtb, dtbT, alog, alogT, d_skip_x, expand2,
                            row(ssm_norm_w[l]), w_br_ssm[l].astype(BF16))
        x = _xattn_merge(x, m_hg, m_ssm, k, v, n_pre, w_q, w_g[2], w_br_xa[l].astype(BF16),
                         w_out[l].astype(BF16), row(norm_mix_post[l]))
        x = _ffn(x.reshape(-1, D_MODEL), row(norm_ffn_pre[l]), w_ffn_up[l].astype(BF16),
                 w_ffn_down[l].astype(BF16), row(norm_ffn_post[l])).reshape(bsz, s, d)
    return x
```

```python
import functools

import jax
import jax.numpy as jnp
from jax import lax
from jax.experimental import pallas as pl
from jax.experimental.pallas import tpu as pltpu

F32 = jnp.float32
BF16 = jnp.bfloat16

EPS = 1e-6
D_MODEL = 1024
N_MEM = 256

HG_HEADS = 8
HG_DK = 128
HG_DV = 128
HG_WIDTH = HG_HEADS * HG_DV
HG_CHUNK = 32

SSM_INNER = 2048
SSM_HEADDIM = 64
SSM_HEADS = 32
SSM_STATE = 128
SSM_GROUPS = 8
SSM_HPG = 4
SSM_CONV = 4
SSM_BC = SSM_GROUPS * SSM_STATE
SSM_CONV_DIM = SSM_INNER + 2 * SSM_BC
SSM_GROUP_W = SSM_HPG * SSM_HEADDIM
LANES = 128
MXU_DEPTH = 256
HEAD_PAD = LANES

XA_HEADS = 4
XA_HEADDIM = 256
XA_WIDTH = 1024

FFN_HIDDEN = 2816

VMEM_LIMIT_BYTES = 56 * 1024 * 1024

HG_SEQ_TILE = 512
SSD_SEQ_TILE = 512
XA_SEQ_TILE = 512
ROW_TILE = 512
HG_SUBTILE = 256
HG_COL_BLOCK = 256
SSD_CHUNK = 128
CUMSUM_BLOCK = 128
SSD_EARLY_GROUPS = 2


def _dot(a, b):
    return jnp.dot(a, b, preferred_element_type=F32)


def _dot_nt(a, b):
    return lax.dot_general(a, b, (((1,), (1,)), ((), ())), preferred_element_type=F32)


def _dot_tn(a, b):
    return lax.dot_general(a, b, (((0,), (0,)), ((), ())), preferred_element_type=F32)


def _split_bf16(v):
    hi = v.astype(BF16)
    return hi, (v - hi.astype(F32)).astype(BF16)


def _rmsnorm(x, w):
    return x * lax.rsqrt(jnp.mean(x * x, axis=-1, keepdims=True) + EPS) * w


def _sigmoid(x):
    return 1.0 / (1.0 + jnp.exp(-x))


def _silu(x):
    return x * _sigmoid(x)


def _softplus(x):
    return jnp.maximum(x, 0.0) + jnp.log1p(jnp.exp(-jnp.abs(x)))


def _chunk_tri(n, chunk, lower):
    r = lax.broadcasted_iota(jnp.int32, (n, n), 0)
    c = lax.broadcasted_iota(jnp.int32, (n, n), 1)
    same = (r // chunk) == (c // chunk)
    tri = (c <= r) if lower else (r <= c)
    return jnp.where(same & tri, 1.0, 0.0).astype(BF16)


def _chunk_cumsum_rows(v, chunk):
    blk = max(CUMSUM_BLOCK, chunk)
    tri = _chunk_tri(blk, chunk, True)
    hi, lo = _split_bf16(v)
    if 2 * blk > MXU_DEPTH:
        return jnp.concatenate(
            [_dot(tri, hi[r0:r0 + blk]) + _dot(tri, lo[r0:r0 + blk]) for r0 in range(0, v.shape[0], blk)], axis=0)
    tri2 = jnp.concatenate([tri, tri], axis=1)
    return jnp.concatenate(
        [_dot(tri2, jnp.concatenate([hi[r0:r0 + blk], lo[r0:r0 + blk]], axis=0))
         for r0 in range(0, v.shape[0], blk)], axis=0)


def _chunk_cumsum_lanes(v, chunk):
    blk = max(CUMSUM_BLOCK, chunk)
    tri = _chunk_tri(blk, chunk, False)
    hi, lo = _split_bf16(v)
    if 2 * blk > MXU_DEPTH:
        return jnp.concatenate(
            [_dot(hi[:, c0:c0 + blk], tri) + _dot(lo[:, c0:c0 + blk], tri) for c0 in range(0, v.shape[1], blk)], axis=1)
    tri2 = jnp.concatenate([tri, tri], axis=0)
    return jnp.concatenate(
        [_dot(jnp.concatenate([hi[:, c0:c0 + blk], lo[:, c0:c0 + blk]], axis=1), tri2)
         for c0 in range(0, v.shape[1], blk)], axis=1)


def _const_spec(shape):
    nd = len(shape)
    return pl.BlockSpec(shape, lambda *_: (0,) * nd, pipeline_mode=pl.Buffered(1))


def _params(n_grid):
    return pltpu.CompilerParams(dimension_semantics=("arbitrary",) * n_grid,
                                vmem_limit_bytes=VMEM_LIMIT_BYTES)


def _mem_kv_kernel(mem_ref, nw_ref, w_ref, k_ref, v_ref):
    mn = _rmsnorm(mem_ref[...], nw_ref[...]).astype(BF16)
    kv = _dot(mn, w_ref[...])
    k_ref[...] = kv[:, :XA_WIDTH].astype(BF16)
    v_ref[...] = kv[:, XA_WIDTH:].astype(BF16)


def _mem_kv(mem2d, norm_w, w_kv):
    rows = mem2d.shape[0]
    tile = min(ROW_TILE, rows)
    out = jax.ShapeDtypeStruct((rows, XA_WIDTH), BF16)
    return pl.pallas_call(
        _mem_kv_kernel,
        grid=(rows // tile,),
        in_specs=[pl.BlockSpec((tile, D_MODEL), lambda i: (i, 0)),
                  _const_spec((1, D_MODEL)),
                  _const_spec((D_MODEL, 2 * XA_WIDTH))],
        out_specs=[pl.BlockSpec((tile, XA_WIDTH), lambda i: (i, 0))] * 2,
        out_shape=[out, out],
        compiler_params=_params(1),
        name="mem_kv",
    )(mem2d, norm_w, w_kv)


_DONE = object()


def _run_skewed(stage_generators):
    waiting = list(stage_generators)
    active = []
    while waiting or active:
        if waiting:
            active.append(waiting.pop(0))
        active = [g for g in active if next(g, _DONE) is not _DONE]


def _hgrn2_kernel(layer, x_ref, nw_ref, w_ref, wg_ref, lbp_ref, gnw_ref, wbr_ref, out_ref,
                  qr_s, kr_s, qi_s, ki_s, v_s, y_s, h_s, o_s, g_s, dch_s, state_s):
    ts = x_ref.shape[1]
    n_chunk = ts // HG_CHUNK
    sub = min(HG_SUBTILE, ts)
    width = HG_COL_BLOCK
    assert n_chunk % 2 == 0 and ts % sub == 0 and ts % CUMSUM_BLOCK == 0 and width % HG_DK == 0

    @pl.when(pl.program_id(1) == 0)
    def _():
        state_s[...] = jnp.zeros_like(state_s)

    h_s[...] = _rmsnorm(x_ref[0], nw_ref[...]).astype(BF16)

    lbp = lbp_ref[...]
    lbe = jnp.exp(lbp - jnp.max(lbp, axis=0, keepdims=True))
    lb_all = jnp.sum(lbe[0:layer + 1], axis=0, keepdims=True) / jnp.sum(lbe, axis=0, keepdims=True)

    tri = _chunk_tri(CUMSUM_BLOCK, HG_CHUNK, True)
    tri2 = jnp.concatenate([tri, tri], axis=1)
    r = lax.broadcasted_iota(jnp.int32, (sub, sub), 0)
    c = lax.broadcasted_iota(jnp.int32, (sub, sub), 1)
    causal = ((r // HG_CHUNK) == (c // HG_CHUNK)) & (c <= r)
    zeros = jnp.zeros((HG_CHUNK, HG_DK), BF16)
    subtiles = [slice(r0, r0 + sub) for r0 in range(0, ts, sub)]
    row_blocks = [slice(r0, r0 + CUMSUM_BLOCK) for r0 in range(0, ts, CUMSUM_BLOCK)]
    chunk_rows = [slice(ci * HG_CHUNK, (ci + 1) * HG_CHUNK) for ci in range(n_chunk)]

    def block_stages(c0):
        cb = slice(c0, c0 + width)
        wcol = lambda section: slice(section * HG_WIDTH + c0, section * HG_WIDTH + c0 + width)
        heads = [slice(c0 + i * HG_DK, c0 + (i + 1) * HG_DK) for i in range(width // HG_DK)]
        pf = _dot(h_s[...], w_ref[:, wcol(1)])
        pq = _dot(h_s[...], w_ref[:, wcol(0)])
        yield
        lb = lb_all[:, cb]
        f = lb + (1.0 - lb) * _sigmoid(pf)
        k = 1.0 - f
        hi, lo = _split_bf16(jnp.log(f))
        yield
        b = jnp.concatenate([_dot(tri2, jnp.concatenate([hi[rb], lo[rb]], axis=0)) for rb in row_blocks], axis=0)
        v_s[:, cb] = _dot(h_s[...], w_ref[:, wcol(2)]).astype(BF16)
        og = _dot(h_s[...], w_ref[:, wcol(3)])
        yield
        b3 = b.reshape(n_chunk, HG_CHUNK, width)
        b_ref = b3[:, HG_CHUNK // 2:HG_CHUNK // 2 + 1]
        b_last = b3[:, HG_CHUNK - 1:HG_CHUNK]
        k3 = k.reshape(n_chunk, HG_CHUNK, width)
        kr_s[:, cb] = (k3 * jnp.exp(b_ref - b3)).reshape(ts, width).astype(BF16)
        ki_s[:, cb] = (k3 * jnp.exp(b_last - b3)).reshape(ts, width).astype(BF16)
        dch_s[:, cb] = jnp.exp(b_last).reshape(n_chunk, width)
        q3 = _silu(pq).reshape(n_chunk, HG_CHUNK, width)
        qr_s[:, cb] = (q3 * jnp.exp(b3 - b_ref)).reshape(ts, width).astype(BF16)
        qi_s[:, cb] = (q3 * jnp.exp(b3)).reshape(ts, width).astype(BF16)
        o_s[:, cb] = _silu(og) * gnw_ref[:, cb]
        yield
        scores, incs = [], []
        for hs in heads:
            scores.append([jnp.where(causal, _dot_nt(qr_s[rs, hs], kr_s[rs, hs]), 0.0).astype(BF16)
                           for rs in subtiles])
            inc_h = []
            for pi in range(n_chunk // 2):
                ra, rb = chunk_rows[2 * pi], chunk_rows[2 * pi + 1]
                vab = jnp.concatenate([jnp.concatenate([v_s[ra, hs], zeros], axis=1),
                                       jnp.concatenate([zeros, v_s[rb, hs]], axis=1)], axis=0)
                inc = _dot_tn(ki_s[ra.start:rb.stop, hs], vab)
                inc_h += [inc[:, :HG_DV], inc[:, HG_DV:]]
            incs.append(inc_h)
        yield
        o_intra, states = [], []
        for hi_, hs in enumerate(heads):
            o_intra.append([_dot(a, v_s[rs, hs]) for a, rs in zip(scores[hi_], subtiles)])
            decay_t = jnp.concatenate([dch_s[:, hs], jnp.zeros((HG_DK - n_chunk, HG_DK), F32)], axis=0).T
            z = hs.start // HG_DK
            st = state_s[z]
            st_h = []
            for ci in range(n_chunk):
                st_h.append(st.astype(BF16))
                st = decay_t[:, ci:ci + 1] * st + incs[hi_][ci]
            state_s[z] = st
            states.append(st_h)
        yield
        for hi_, hs in enumerate(heads):
            o_inter = []
            for pi in range(n_chunk // 2):
                ra, rb = chunk_rows[2 * pi], chunk_rows[2 * pi + 1]
                pair = _dot(qi_s[ra.start:rb.stop, hs],
                            jnp.concatenate([states[hi_][2 * pi], states[hi_][2 * pi + 1]], axis=1))
                o_inter += [pair[:HG_CHUNK, :HG_DV], pair[HG_CHUNK:, HG_DV:]]
            o = jnp.concatenate(o_intra[hi_], axis=0) + jnp.concatenate(o_inter, axis=0)
            y_s[:, hs] = (o * lax.rsqrt(jnp.mean(o * o, axis=-1, keepdims=True) + EPS) * o_s[:, hs]).astype(BF16)

    def gate_stages():
        for c0 in range(0, D_MODEL, width):
            g_s[:, c0:c0 + width] = _sigmoid(_dot(h_s[...], wg_ref[:, c0:c0 + width]))
            yield

    _run_skewed([block_stages(c0) for c0 in range(0, HG_WIDTH, width)] + [gate_stages()])
    out_ref[0] = g_s[...] * _dot(y_s[...], wbr_ref[...])


def _hgrn2_branch(layer, x, norm_w, w_hg, w_gate, lb_param, gn_w, w_br):
    bsz, s, _ = x.shape
    ts = min(HG_SEQ_TILE, s)
    tile = lambda: pl.BlockSpec((1, ts, D_MODEL), lambda b, i: (b, i, 0))
    return pl.pallas_call(
        functools.partial(_hgrn2_kernel, layer),
        grid=(bsz, s // ts),
        in_specs=[tile(),
                  _const_spec((1, D_MODEL)),
                  _const_spec((D_MODEL, 4 * HG_WIDTH)),
                  _const_spec((D_MODEL, D_MODEL)),
                  _const_spec(lb_param.shape),
                  _const_spec((1, HG_WIDTH)),
                  _const_spec((HG_WIDTH, D_MODEL))],
        out_specs=tile(),
        out_shape=jax.ShapeDtypeStruct(x.shape, F32),
        scratch_shapes=[pltpu.VMEM((ts, HG_WIDTH), BF16)] * 7
                       + [pltpu.VMEM((ts, HG_WIDTH), F32)] * 2
                       + [pltpu.VMEM((ts // HG_CHUNK, HG_WIDTH), F32),
                          pltpu.VMEM((HG_HEADS, HG_DV, HG_DK), F32)],
        compiler_params=_params(2),
        name="hgrn2_branch",
    )(x, norm_w, w_hg, w_gate, lb_param, gn_w, w_br)


def _expand_heads(v, expand2):
    hi, lo = _split_bf16(v)
    return _dot(jnp.concatenate([hi, lo], axis=1), expand2)


def _ssd_kernel(x_ref, nw_ref, wz_ref, wxbc_ref, wdt_ref, wdtT_ref, wg_ref, convw_ref, convb_ref,
                dtb_ref, dtbT_ref, a_ref, aT_ref, dskip_ref, expand2_ref, normw_ref, wbr_ref, out_ref,
                raw_s, h_s, yn_s, g_s, state_s):
    ts = x_ref.shape[1]
    lc = SSD_CHUNK
    n_chunk = ts // lc
    pad = 8
    gw = SSM_GROUP_W + 2 * SSM_STATE
    slabs_per_group = gw // LANES

    @pl.when(pl.program_id(1) == 0)
    def _():
        state_s[...] = jnp.zeros_like(state_s)
        raw_s[:, 0:pad, :] = jnp.zeros((SSM_CONV_DIM // LANES, pad, LANES), F32)

    h_s[...] = _rmsnorm(x_ref[0], nw_ref[...]).astype(BF16)

    r = lax.broadcasted_iota(jnp.int32, (lc, lc), 0)
    c = lax.broadcasted_iota(jnp.int32, (lc, lc), 1)
    causal = c <= r
    head_in_group = lax.broadcasted_iota(jnp.int32, (1, SSM_GROUP_W), 1) // SSM_HEADDIM
    chunks = [slice(ci * lc, (ci + 1) * lc) for ci in range(n_chunk)]

    def group_stages(gi):
        ws = slice(gi * SSM_GROUP_W, (gi + 1) * SSM_GROUP_W)
        pcols = slice(gi * gw, (gi + 1) * gw)
        raw = _dot(h_s[...], wxbc_ref[:, pcols])
        yield
        din = _dot(din_hl, expand2_ref[:, ws])
        dtd = _dot(dtd_hl, expand2_ref[:, ws])
        cols = []
        for si in range(slabs_per_group):
            ct = gi * slabs_per_group + si
            ls = slice(si * LANES, (si + 1) * LANES)
            pls = slice(gi * gw + si * LANES, gi * gw + (si + 1) * LANES)
            raw_s[ct, pad:pad + ts, :] = raw[:, ls]
            acc = convb_ref[:, pls] + convw_ref[SSM_CONV - 1:SSM_CONV, pls] * raw[:, ls]
            for j in range(SSM_CONV - 1):
                off = pad - (SSM_CONV - 1) + j
                acc = acc + convw_ref[j:j + 1, pls] * raw_s[ct, off:off + ts, :]
            raw_s[ct, 0:pad, :] = raw_s[ct, ts:ts + pad, :]
            cols.append(_silu(acc))
        xs = jnp.concatenate(cols[:SSM_GROUP_W // LANES], axis=1)
        bm = cols[-2].astype(BF16)
        cm = cols[-1].astype(BF16)
        xd = (xs * dtd).astype(BF16)
        xs_bf = xs.astype(BF16)
        y0 = dskip_ref[:, ws] * xs
        yield
        incs = [_dot_tn(bm[rs], xd[rs]) for rs in chunks]
        cbs = [_dot_nt(cm[rs], bm[rs]) for rs in chunks]
        zp = _dot(h_s[...], wz_ref[:, ws])
        yield
        st = state_s[gi]
        states = []
        for ci in range(n_chunk):
            states.append(st.astype(BF16))
            st = din[ci * lc + lc - 1:ci * lc + lc] * st + incs[ci]
        state_s[gi] = st
        ms = []
        for ci, rs in enumerate(chunks):
            for j in range(SSM_HPG):
                hd = gi * SSM_HPG + j
                seg = cum[rs, hd:hd + 1] - cumT[hd:hd + 1, rs]
                ms.append(jnp.where(causal, cbs[ci] * jnp.exp(seg) * dtT[hd:hd + 1, rs], 0.0).astype(BF16))
        yield
        ys = []
        for ci, rs in enumerate(chunks):
            m_all = jnp.concatenate(ms[ci * SSM_HPG:(ci + 1) * SSM_HPG], axis=1)
            x_all = jnp.concatenate([jnp.where(head_in_group == j, xs_bf[rs], jnp.zeros_like(xs_bf[rs]))
                                     for j in range(SSM_HPG)], axis=0)
            ys.append(_dot(cm[rs], states[ci]) * din[rs] + _dot(m_all, x_all))
        yield
        y = (y0 + jnp.concatenate(ys, axis=0)) * _silu(zp)
        yn_s[:, ws] = (y * lax.rsqrt(jnp.mean(y * y, axis=-1, keepdims=True) + EPS) * normw_ref[:, ws]).astype(BF16)

    def gate_stages():
        for c0 in range(0, D_MODEL, SSM_GROUP_W):
            g_s[:, c0:c0 + SSM_GROUP_W] = _sigmoid(_dot(h_s[...], wg_ref[:, c0:c0 + SSM_GROUP_W]))
            yield

    groups = [group_stages(gi) for gi in range(SSM_GROUPS)]
    for gen in groups[:SSD_EARLY_GROUPS]:
        next(gen)

    a = -jnp.exp(a_ref[...])
    dt = _softplus(_dot(h_s[...], wdt_ref[...]) + dtb_ref[...])
    cum = _chunk_cumsum_rows(dt * a, lc)
    aT = -jnp.exp(aT_ref[...])
    dtT = _softplus(_dot_nt(wdtT_ref[...], h_s[...]) + dtbT_ref[...])
    cumT = _chunk_cumsum_lanes(dtT * aT, lc)
    cum3 = cum.reshape(n_chunk, lc, HEAD_PAD)
    dout = jnp.exp(cum3[:, lc - 1:lc] - cum3).reshape(ts, HEAD_PAD)
    din_hl = jnp.concatenate(_split_bf16(jnp.exp(cum)), axis=1)
    dtd_hl = jnp.concatenate(_split_bf16(dt * dout), axis=1)

    _run_skewed(groups + [gate_stages()])
    out_ref[0] = g_s[...] * _dot(yn_s[...], wbr_ref[...])


def _ssd_branch(x, norm_w, w_z, w_xbc, w_dt, w_dtT, w_gate, conv_w, conv_b, dt_bias, dt_biasT,
                a_log, a_logT, d_skip_x, expand2, ssm_norm_w, w_br):
    bsz, s, _ = x.shape
    ts = min(SSD_SEQ_TILE, s)
    assert ts % SSD_CHUNK == 0 and SSD_CHUNK % CUMSUM_BLOCK == 0
    tile = lambda: pl.BlockSpec((1, ts, D_MODEL), lambda b, i: (b, i, 0))
    consts = [norm_w, w_z, w_xbc, w_dt, w_dtT, w_gate, conv_w, conv_b, dt_bias, dt_biasT,
              a_log, a_logT, d_skip_x, expand2, ssm_norm_w, w_br]
    return pl.pallas_call(
        _ssd_kernel,
        grid=(bsz, s // ts),
        in_specs=[tile()] + [_const_spec(c.shape) for c in consts],
        out_specs=tile(),
        out_shape=jax.ShapeDtypeStruct(x.shape, F32),
        scratch_shapes=[pltpu.VMEM((SSM_CONV_DIM // LANES, ts + 8, LANES), F32),
                        pltpu.VMEM((ts, D_MODEL), BF16),
                        pltpu.VMEM((ts, SSM_INNER), BF16),
                        pltpu.VMEM((ts, D_MODEL), F32),
                        pltpu.VMEM((SSM_GROUPS, SSM_STATE, SSM_GROUP_W), F32)],
        compiler_params=_params(2),
        name="ssd_branch",
    )(x, *consts)


def _group_xbc_columns():
    cols = []
    for gi in range(SSM_GROUPS):
        cols += list(range(gi * SSM_GROUP_W, (gi + 1) * SSM_GROUP_W))
        cols += list(range(SSM_INNER + gi * SSM_STATE, SSM_INNER + (gi + 1) * SSM_STATE))
        cols += list(range(SSM_INNER + SSM_BC + gi * SSM_STATE, SSM_INNER + SSM_BC + (gi + 1) * SSM_STATE))
    return jnp.asarray(cols, dtype=jnp.int32)


def _xattn_kernel(x_ref, mhg_ref, mssm_ref, k_ref, v_ref, nw_ref, wq_ref, wg_ref, wbr_ref,
                  wout_ref, npost_ref, out_ref, o_s):
    x = x_ref[0]
    h = _rmsnorm(x, nw_ref[...]).astype(BF16)
    q = _dot(h, wq_ref[...]).astype(BF16)
    for z in range(XA_HEADS):
        hs = slice(z * XA_HEADDIM, (z + 1) * XA_HEADDIM)
        sc = _dot_nt(q[:, hs], k_ref[0, :, hs]) * (XA_HEADDIM ** -0.5)
        e = jnp.exp(sc - jnp.max(sc, axis=-1, keepdims=True))
        p = e / jnp.sum(e, axis=-1, keepdims=True)
        o_s[:, hs] = _dot(p.astype(BF16), v_ref[0, :, hs]).astype(BF16)
    gate = _sigmoid(_dot(h, wg_ref[...]))
    merged = mhg_ref[0] + mssm_ref[0] + gate * _dot(o_s[...], wbr_ref[...])
    out_ref[0] = x + _rmsnorm(_dot(merged.astype(BF16), wout_ref[...]), npost_ref[...])


def _xattn_merge(x, m_hg, m_ssm, k, v, norm_w, w_q, w_gate, w_br, w_out, norm_post):
    bsz, s, _ = x.shape
    ts = min(XA_SEQ_TILE, s)
    tile = lambda: pl.BlockSpec((1, ts, D_MODEL), lambda b, i: (b, i, 0))
    kv = lambda: pl.BlockSpec((1, N_MEM, XA_WIDTH), lambda b, i: (b, 0, 0))
    sq = lambda: _const_spec((D_MODEL, D_MODEL))
    return pl.pallas_call(
        _xattn_kernel,
        grid=(bsz, s // ts),
        in_specs=[tile(), tile(), tile(), kv(), kv(), _const_spec((1, D_MODEL)),
                  sq(), sq(), sq(), sq(), _const_spec((1, D_MODEL))],
        out_specs=tile(),
        out_shape=jax.ShapeDtypeStruct(x.shape, F32),
        scratch_shapes=[pltpu.VMEM((ts, XA_WIDTH), BF16)],
        compiler_params=_params(2),
        name="xattn_merge",
    )(x, m_hg, m_ssm, k, v, norm_w, w_q, w_gate, w_br, w_out, norm_post)


def _ffn_kernel(x_ref, npre_ref, wup_ref, wdown_ref, npost_ref, out_ref):
    x = x_ref[...]
    h = _rmsnorm(x, npre_ref[...]).astype(BF16)
    g = _dot(h, wup_ref[:, :FFN_HIDDEN])
    u = _dot(h, wup_ref[:, FFN_HIDDEN:])
    act = (_silu(g) * u).astype(BF16)
    out_ref[...] = x + _rmsnorm(_dot(act, wdown_ref[...]), npost_ref[...])


def _ffn(x2d, norm_pre, w_up, w_down, norm_post):
    rows = x2d.shape[0]
    tile = min(ROW_TILE, rows)
    spec = lambda: pl.BlockSpec((tile, D_MODEL), lambda i: (i, 0))
    return pl.pallas_call(
        _ffn_kernel,
        grid=(rows // tile,),
        in_specs=[spec(), _const_spec((1, D_MODEL)), _const_spec((D_MODEL, 2 * FFN_HIDDEN)),
                  _const_spec((FFN_HIDDEN, D_MODEL)), _const_spec((1, D_MODEL))],
        out_specs=spec(),
        out_shape=jax.ShapeDtypeStruct(x2d.shape, F32),
        compiler_params=_params(1),
        name="swiglu_ffn",
    )(x2d, norm_pre, w_up, w_down, norm_post)


def _pad_heads(v, fill=0.0):
    row = jnp.full((1, HEAD_PAD), fill, F32).at[0, :SSM_HEADS].set(v.astype(F32))
    return row, row.reshape(HEAD_PAD, 1)


def kernel(x, mem, norm_mix_pre, norm_mix_post, norm_ffn_pre, norm_ffn_post, norm_mem, w_in,
           hg_lb_param, hg_norm_w, conv_w, conv_b, dt_bias, a_log, d_skip, ssm_norm_w, w_mem_kv,
           w_br_hg, w_br_ssm, w_br_xa, w_out, w_ffn_up, w_ffn_down):
    bsz, s, d = x.shape
    depth = w_in.shape[0]
    assert d == D_MODEL and all(s % min(t, s) == 0 for t in (HG_SEQ_TILE, SSD_SEQ_TILE, XA_SEQ_TILE))
    row = lambda v: v.reshape(1, -1).astype(F32)

    expand = (jnp.arange(HEAD_PAD)[:, None] == (jnp.arange(SSM_INNER)[None, :] // SSM_HEADDIM)).astype(BF16)
    expand2 = jnp.concatenate([expand, expand], axis=0)

    xbc_cols = _group_xbc_columns()
    o_hg = 4 * HG_WIDTH
    o_z, o_xbc = o_hg, o_hg + SSM_INNER
    o_dt = o_xbc + SSM_CONV_DIM
    o_q = o_dt + SSM_HEADS
    o_gate = o_q + XA_WIDTH

    for l in range(depth):
        w = w_in[l]
        w_hg = w[:, :o_hg].astype(BF16)
        w_z = w[:, o_z:o_xbc].astype(BF16)
        w_xbc = w[:, o_xbc:o_dt][:, xbc_cols].astype(BF16)
        w_dt = jnp.zeros((D_MODEL, HEAD_PAD), BF16).at[:, :SSM_HEADS].set(w[:, o_dt:o_q].astype(BF16))
        w_q = w[:, o_q:o_gate].astype(BF16)
        w_g = [w[:, o_gate + i * D_MODEL:o_gate + (i + 1) * D_MODEL].astype(BF16) for i in range(3)]
        dtb, dtbT = _pad_heads(dt_bias[l])
        alog, alogT = _pad_heads(a_log[l])
        d_skip_x = jnp.repeat(d_skip[l].astype(F32), SSM_HEADDIM).reshape(1, SSM_INNER)
        n_pre = row(norm_mix_pre[l])

        k, v = _mem_kv(mem.reshape(-1, D_MODEL), row(norm_mem[l]), w_mem_kv[l].astype(BF16))
        k = k.reshape(bsz, -1, XA_WIDTH)
        v = v.reshape(bsz, -1, XA_WIDTH)
        m_hg = _hgrn2_branch(l, x, n_pre, w_hg, w_g[0], hg_lb_param.astype(F32),
                             row(hg_norm_w[l]), w_br_hg[l].astype(BF16))
        m_ssm = _ssd_branch(x, n_pre, w_z, w_xbc, w_dt, w_dt.T, w_g[1], conv_w[l].astype(F32)[:, xbc_cols],
                            row(conv_b[l])[:, xbc_cols], dtb, dtbT, alog, alogT, d_skip_x, expand2,
                            row(ssm_norm_w[l]), w_br_ssm[l].astype(BF16))
        x = _xattn_merge(x, m_hg, m_ssm, k, v, n_pre, w_q, w_g[2], w_br_xa[l].astype(BF16),
                         w_out[l].astype(BF16), row(norm_mix_post[l]))
        x = _ffn(x.reshape(-1, D_MODEL), row(norm_ffn_pre[l]), w_ffn_up[l].astype(BF16),
                 w_ffn_down[l].astype(BF16), row(norm_ffn_post[l])).reshape(bsz, s, d)
    return x
```

```python
import functools

import jax
import jax.numpy as jnp
from jax import lax
from jax.experimental import pallas as pl
from jax.experimental.pallas import tpu as pltpu

F32 = jnp.float32
BF16 = jnp.bfloat16

EPS = 1e-6
D_MODEL = 1024
N_MEM = 256

HG_HEADS = 8
HG_DK = 128
HG_DV = 128
HG_WIDTH = HG_HEADS * HG_DV
HG_CHUNK = 32

SSM_INNER = 2048
SSM_HEADDIM = 64
SSM_HEADS = 32
SSM_STATE = 128
SSM_GROUPS = 8
SSM_HPG = 4
SSM_CONV = 4
SSM_BC = SSM_GROUPS * SSM_STATE
SSM_CONV_DIM = SSM_INNER + 2 * SSM_BC
SSM_GROUP_W = SSM_HPG * SSM_HEADDIM
LANES = 128
SUBLANES = 8
MXU_DEPTH = 256
HEAD_PAD = LANES

XA_HEADS = 4
XA_HEADDIM = 256
XA_WIDTH = 1024

FFN_HIDDEN = 2816

VMEM_LIMIT_BYTES = 56 * 1024 * 1024

HG_SEQ_TILE = 256
SSD_SEQ_TILE = 512
XA_SEQ_TILE = 512
ROW_TILE = 512
HG_SUBTILE = 128
HG_COL_BLOCK = 256
SSD_CHUNK = 128
CUMSUM_BLOCK = 128
SSD_EARLY_GROUPS = 3


def _dot(a, b):
    return jnp.dot(a, b, preferred_element_type=F32)


def _dot_nt(a, b):
    return lax.dot_general(a, b, (((1,), (1,)), ((), ())), preferred_element_type=F32)


def _dot_tn(a, b):
    return lax.dot_general(a, b, (((0,), (0,)), ((), ())), preferred_element_type=F32)


def _split_bf16(v):
    hi = v.astype(BF16)
    return hi, (v - hi.astype(F32)).astype(BF16)


def _rmsnorm(x, w):
    return x * lax.rsqrt(jnp.mean(x * x, axis=-1, keepdims=True) + EPS) * w


def _sigmoid(x):
    return 1.0 / (1.0 + jnp.exp(-x))


def _silu(x):
    return x * _sigmoid(x)


def _softplus(x):
    return jnp.maximum(x, 0.0) + jnp.log1p(jnp.exp(-jnp.abs(x)))


def _chunk_tri(n, chunk, lower):
    r = lax.broadcasted_iota(jnp.int32, (n, n), 0)
    c = lax.broadcasted_iota(jnp.int32, (n, n), 1)
    same = (r // chunk) == (c // chunk)
    tri = (c <= r) if lower else (r <= c)
    return jnp.where(same & tri, 1.0, 0.0).astype(BF16)


def _chunk_cumsum_rows(v, chunk):
    blk = max(CUMSUM_BLOCK, chunk)
    tri = _chunk_tri(blk, chunk, True)
    hi, lo = _split_bf16(v)
    if 2 * blk > MXU_DEPTH:
        return jnp.concatenate(
            [_dot(tri, hi[r0:r0 + blk]) + _dot(tri, lo[r0:r0 + blk]) for r0 in range(0, v.shape[0], blk)], axis=0)
    tri2 = jnp.concatenate([tri, tri], axis=1)
    return jnp.concatenate(
        [_dot(tri2, jnp.concatenate([hi[r0:r0 + blk], lo[r0:r0 + blk]], axis=0))
         for r0 in range(0, v.shape[0], blk)], axis=0)


def _chunk_cumsum_lanes(v, chunk):
    blk = max(CUMSUM_BLOCK, chunk)
    tri = _chunk_tri(blk, chunk, False)
    hi, lo = _split_bf16(v)
    if 2 * blk > MXU_DEPTH:
        return jnp.concatenate(
            [_dot(hi[:, c0:c0 + blk], tri) + _dot(lo[:, c0:c0 + blk], tri) for c0 in range(0, v.shape[1], blk)], axis=1)
    tri2 = jnp.concatenate([tri, tri], axis=0)
    return jnp.concatenate(
        [_dot(jnp.concatenate([hi[:, c0:c0 + blk], lo[:, c0:c0 + blk]], axis=1), tri2)
         for c0 in range(0, v.shape[1], blk)], axis=1)


def _const_spec(shape):
    nd = len(shape)
    return pl.BlockSpec(shape, lambda *_: (0,) * nd, pipeline_mode=pl.Buffered(1))


def _params(n_grid):
    return pltpu.CompilerParams(dimension_semantics=("arbitrary",) * n_grid,
                                vmem_limit_bytes=VMEM_LIMIT_BYTES)


def _mem_kv_kernel(mem_ref, nw_ref, w_ref, k_ref, v_ref):
    mn = _rmsnorm(mem_ref[...], nw_ref[...]).astype(BF16)
    kv = _dot(mn, w_ref[...])
    k_ref[...] = kv[:, :XA_WIDTH].astype(BF16)
    v_ref[...] = kv[:, XA_WIDTH:].astype(BF16)


def _mem_kv(mem2d, norm_w, w_kv):
    rows = mem2d.shape[0]
    tile = min(ROW_TILE, rows)
    out = jax.ShapeDtypeStruct((rows, XA_WIDTH), BF16)
    return pl.pallas_call(
        _mem_kv_kernel,
        grid=(rows // tile,),
        in_specs=[pl.BlockSpec((tile, D_MODEL), lambda i: (i, 0)),
                  _const_spec((1, D_MODEL)),
                  _const_spec((D_MODEL, 2 * XA_WIDTH))],
        out_specs=[pl.BlockSpec((tile, XA_WIDTH), lambda i: (i, 0))] * 2,
        out_shape=[out, out],
        compiler_params=_params(1),
        name="mem_kv",
    )(mem2d, norm_w, w_kv)


_DONE = object()


def _run_skewed(stage_generators):
    waiting = list(stage_generators)
    active = []
    while waiting or active:
        if waiting:
            active.append(waiting.pop(0))
        active = [g for g in active if next(g, _DONE) is not _DONE]


def _hgrn2_kernel(layer, x_ref, nw_ref, w_ref, wg_ref, lbp_ref, gnw_ref, wbr_ref, out_ref,
                  qr_s, kr_s, qi_s, ki_s, v_s, y_s, h_s, o_s, g_s, dch_s, state_s):
    ts = x_ref.shape[1]
    n_chunk = ts // HG_CHUNK
    sub = min(HG_SUBTILE, ts)
    width = HG_COL_BLOCK
    assert n_chunk % 2 == 0 and ts % sub == 0 and ts % CUMSUM_BLOCK == 0 and width % HG_DK == 0

    @pl.when(pl.program_id(1) == 0)
    def _():
        state_s[...] = jnp.zeros_like(state_s)

    h_s[...] = _rmsnorm(x_ref[0], nw_ref[...]).astype(BF16)

    lbp = lbp_ref[...]
    lbe = jnp.exp(lbp - jnp.max(lbp, axis=0, keepdims=True))
    lb_all = jnp.sum(lbe[0:layer + 1], axis=0, keepdims=True) / jnp.sum(lbe, axis=0, keepdims=True)

    tri = _chunk_tri(CUMSUM_BLOCK, HG_CHUNK, True)
    tri2 = jnp.concatenate([tri, tri], axis=1)
    r = lax.broadcasted_iota(jnp.int32, (sub, sub), 0)
    c = lax.broadcasted_iota(jnp.int32, (sub, sub), 1)
    causal = ((r // HG_CHUNK) == (c // HG_CHUNK)) & (c <= r)
    zeros = jnp.zeros((HG_CHUNK, HG_DK), BF16)
    subtiles = [slice(r0, r0 + sub) for r0 in range(0, ts, sub)]
    row_blocks = [slice(r0, r0 + CUMSUM_BLOCK) for r0 in range(0, ts, CUMSUM_BLOCK)]
    chunk_rows = [slice(ci * HG_CHUNK, (ci + 1) * HG_CHUNK) for ci in range(n_chunk)]

    def block_stages(c0):
        cb = slice(c0, c0 + width)
        wcol = lambda section: slice(section * HG_WIDTH + c0, section * HG_WIDTH + c0 + width)
        heads = [slice(c0 + i * HG_DK, c0 + (i + 1) * HG_DK) for i in range(width // HG_DK)]
        pf = _dot(h_s[...], w_ref[:, wcol(1)])
        pq = _dot(h_s[...], w_ref[:, wcol(0)])
        yield
        lb = lb_all[:, cb]
        f = lb + (1.0 - lb) * _sigmoid(pf)
        k = 1.0 - f
        hi, lo = _split_bf16(jnp.log(f))
        yield
        b = jnp.concatenate([_dot(tri2, jnp.concatenate([hi[rb], lo[rb]], axis=0)) for rb in row_blocks], axis=0)
        v_s[:, cb] = _dot(h_s[...], w_ref[:, wcol(2)]).astype(BF16)
        og = _dot(h_s[...], w_ref[:, wcol(3)])
        yield
        b3 = b.reshape(n_chunk, HG_CHUNK, width)
        b_ref = b3[:, HG_CHUNK // 2:HG_CHUNK // 2 + 1]
        b_last = b3[:, HG_CHUNK - 1:HG_CHUNK]
        k3 = k.reshape(n_chunk, HG_CHUNK, width)
        kr_s[:, cb] = (k3 * jnp.exp(b_ref - b3)).reshape(ts, width).astype(BF16)
        ki_s[:, cb] = (k3 * jnp.exp(b_last - b3)).reshape(ts, width).astype(BF16)
        dch_s[:, cb] = jnp.exp(b_last).reshape(n_chunk, width)
        q3 = _silu(pq).reshape(n_chunk, HG_CHUNK, width)
        qr_s[:, cb] = (q3 * jnp.exp(b3 - b_ref)).reshape(ts, width).astype(BF16)
        qi_s[:, cb] = (q3 * jnp.exp(b3)).reshape(ts, width).astype(BF16)
        o_s[:, cb] = _silu(og) * gnw_ref[:, cb]
        yield
        scores, incs = [], []
        for hs in heads:
            scores.append([jnp.where(causal, _dot_nt(qr_s[rs, hs], kr_s[rs, hs]), 0.0).astype(BF16)
                           for rs in subtiles])
            inc_h = []
            for pi in range(n_chunk // 2):
                ra, rb = chunk_rows[2 * pi], chunk_rows[2 * pi + 1]
                vab = jnp.concatenate([jnp.concatenate([v_s[ra, hs], zeros], axis=1),
                                       jnp.concatenate([zeros, v_s[rb, hs]], axis=1)], axis=0)
                inc = _dot_tn(ki_s[ra.start:rb.stop, hs], vab)
                inc_h += [inc[:, :HG_DV], inc[:, HG_DV:]]
            incs.append(inc_h)
        yield
        o_intra, states = [], []
        for hi_, hs in enumerate(heads):
            o_intra.append([_dot(a, v_s[rs, hs]) for a, rs in zip(scores[hi_], subtiles)])
            decay_t = jnp.concatenate([dch_s[:, hs], jnp.zeros((HG_DK - n_chunk, HG_DK), F32)], axis=0).T
            z = hs.start // HG_DK
            st = state_s[z]
            st_h = []
            for ci in range(n_chunk):
                st_h.append(st.astype(BF16))
                st = decay_t[:, ci:ci + 1] * st + incs[hi_][ci]
            state_s[z] = st
            states.append(st_h)
        yield
        for hi_, hs in enumerate(heads):
            o_inter = []
            for pi in range(n_chunk // 2):
                ra, rb = chunk_rows[2 * pi], chunk_rows[2 * pi + 1]
                pair = _dot(qi_s[ra.start:rb.stop, hs],
                            jnp.concatenate([states[hi_][2 * pi], states[hi_][2 * pi + 1]], axis=1))
                o_inter += [pair[:HG_CHUNK, :HG_DV], pair[HG_CHUNK:, HG_DV:]]
            o = jnp.concatenate(o_intra[hi_], axis=0) + jnp.concatenate(o_inter, axis=0)
            y_s[:, hs] = (o * lax.rsqrt(jnp.mean(o * o, axis=-1, keepdims=True) + EPS) * o_s[:, hs]).astype(BF16)

    def gate_stages():
        for c0 in range(0, D_MODEL, width):
            g_s[:, c0:c0 + width] = _sigmoid(_dot(h_s[...], wg_ref[:, c0:c0 + width]))
            yield

    _run_skewed([block_stages(c0) for c0 in range(0, HG_WIDTH, width)] + [gate_stages()])
    out_ref[0] = g_s[...] * _dot(y_s[...], wbr_ref[...])


def _hgrn2_branch(layer, x, norm_w, w_hg, w_gate, lb_param, gn_w, w_br):
    bsz, s, _ = x.shape
    ts = min(HG_SEQ_TILE, s)
    tile = lambda: pl.BlockSpec((1, ts, D_MODEL), lambda b, i: (b, i, 0))
    return pl.pallas_call(
        functools.partial(_hgrn2_kernel, layer),
        grid=(bsz, s // ts),
        in_specs=[tile(),
                  _const_spec((1, D_MODEL)),
                  _const_spec((D_MODEL, 4 * HG_WIDTH)),
                  _const_spec((D_MODEL, D_MODEL)),
                  _const_spec(lb_param.shape),
                  _const_spec((1, HG_WIDTH)),
                  _const_spec((HG_WIDTH, D_MODEL))],
        out_specs=tile(),
        out_shape=jax.ShapeDtypeStruct(x.shape, F32),
        scratch_shapes=[pltpu.VMEM((ts, HG_WIDTH), BF16)] * 7
                       + [pltpu.VMEM((ts, HG_WIDTH), F32)] * 2
                       + [pltpu.VMEM((ts // HG_CHUNK, HG_WIDTH), F32),
                          pltpu.VMEM((HG_HEADS, HG_DV, HG_DK), F32)],
        compiler_params=_params(2),
        name="hgrn2_branch",
    )(x, norm_w, w_hg, w_gate, lb_param, gn_w, w_br)


def _ssd_kernel(x_ref, nw_ref, wz_ref, wxbc_ref, wdt_ref, wdtT_ref, wg_ref, convw_ref, convb_ref,
                dtb_ref, dtbT_ref, a_ref, aT_ref, dskip_ref, expand2_ref, normw_ref, wbr_ref, out_ref,
                raw_s, h_s, yn_s, g_s, state_s):
    ts = x_ref.shape[1]
    lc = SSD_CHUNK
    n_chunk = ts // lc
    pad = SUBLANES
    gw = SSM_GROUP_W + 2 * SSM_STATE
    slabs_per_group = gw // LANES

    @pl.when(pl.program_id(1) == 0)
    def _():
        state_s[...] = jnp.zeros_like(state_s)
        raw_s[:, 0:pad, :] = jnp.zeros((SSM_CONV_DIM // LANES, pad, LANES), F32)

    h_s[...] = _rmsnorm(x_ref[0], nw_ref[...]).astype(BF16)

    r = lax.broadcasted_iota(jnp.int32, (lc, lc), 0)
    c = lax.broadcasted_iota(jnp.int32, (lc, lc), 1)
    causal = c <= r
    head_in_group = lax.broadcasted_iota(jnp.int32, (1, SSM_GROUP_W), 1) // SSM_HEADDIM
    chunks = [slice(ci * lc, (ci + 1) * lc) for ci in range(n_chunk)]

    def group_stages(gi):
        ws = slice(gi * SSM_GROUP_W, (gi + 1) * SSM_GROUP_W)
        pcols = slice(gi * gw, (gi + 1) * gw)
        raw = _dot(h_s[...], wxbc_ref[:, pcols])
        yield
        din = _dot(din_hl, expand2_ref[:, ws])
        dtd = _dot(dtd_hl, expand2_ref[:, ws])
        cols = []
        for si in range(slabs_per_group):
            ct = gi * slabs_per_group + si
            ls = slice(si * LANES, (si + 1) * LANES)
            pls = slice(gi * gw + si * LANES, gi * gw + (si + 1) * LANES)
            raw_s[ct, pad:pad + ts, :] = raw[:, ls]
            acc = convb_ref[:, pls] + convw_ref[SSM_CONV - 1:SSM_CONV, pls] * raw[:, ls]
            for j in range(SSM_CONV - 1):
                off = pad - (SSM_CONV - 1) + j
                acc = acc + convw_ref[j:j + 1, pls] * raw_s[ct, off:off + ts, :]
            raw_s[ct, 0:pad, :] = raw_s[ct, ts:ts + pad, :]
            cols.append(_silu(acc))
        xs = jnp.concatenate(cols[:SSM_GROUP_W // LANES], axis=1)
        bm = cols[-2].astype(BF16)
        cm = cols[-1].astype(BF16)
        xd = (xs * dtd).astype(BF16)
        xs_bf = xs.astype(BF16)
        y0 = dskip_ref[:, ws] * xs
        yield
        incs = [_dot_tn(bm[rs], xd[rs]) for rs in chunks]
        cbs = [_dot_nt(cm[rs], bm[rs]) for rs in chunks]
        zp = _dot(h_s[...], wz_ref[:, ws])
        yield
        st = state_s[gi]
        states = []
        for ci in range(n_chunk):
            states.append(st.astype(BF16))
            st = din[ci * lc + lc - 1:ci * lc + lc] * st + incs[ci]
        state_s[gi] = st
        ms = []
        for ci, rs in enumerate(chunks):
            for j in range(SSM_HPG):
                hd = gi * SSM_HPG + j
                seg = cum[rs, hd:hd + 1] - cumT[hd:hd + 1, rs]
                ms.append(jnp.where(causal, cbs[ci] * jnp.exp(seg) * dtT[hd:hd + 1, rs], 0.0).astype(BF16))
        yield
        ys = []
        for ci, rs in enumerate(chunks):
            m_all = jnp.concatenate(ms[ci * SSM_HPG:(ci + 1) * SSM_HPG], axis=1)
            x_all = jnp.concatenate([jnp.where(head_in_group == j, xs_bf[rs], jnp.zeros_like(xs_bf[rs]))
                                     for j in range(SSM_HPG)], axis=0)
            ys.append(_dot(cm[rs], states[ci]) * din[rs] + _dot(m_all, x_all))
        yield
        y = (y0 + jnp.concatenate(ys, axis=0)) * _silu(zp)
        yn_s[:, ws] = (y * lax.rsqrt(jnp.mean(y * y, axis=-1, keepdims=True) + EPS) * normw_ref[:, ws]).astype(BF16)

    def gate_stages():
        for c0 in range(0, D_MODEL, SSM_GROUP_W):
            g_s[:, c0:c0 + SSM_GROUP_W] = _sigmoid(_dot(h_s[...], wg_ref[:, c0:c0 + SSM_GROUP_W]))
            yield

    groups = [group_stages(gi) for gi in range(SSM_GROUPS)]
    for gen in groups[:SSD_EARLY_GROUPS]:
        next(gen)

    a = -jnp.exp(a_ref[...])
    dt = _softplus(_dot(h_s[...], wdt_ref[...]) + dtb_ref[...])
    cum = _chunk_cumsum_rows(dt * a, lc)
    aT = -jnp.exp(aT_ref[...])
    dtT = _softplus(_dot_nt(wdtT_ref[...], h_s[...]) + dtbT_ref[...])
    cumT = _chunk_cumsum_lanes(dtT * aT, lc)
    cum3 = cum.reshape(n_chunk, lc, HEAD_PAD)
    dout = jnp.exp(cum3[:, lc - 1:lc] - cum3).reshape(ts, HEAD_PAD)
    din_hl = jnp.concatenate(_split_bf16(jnp.exp(cum)), axis=1)
    dtd_hl = jnp.concatenate(_split_bf16(dt * dout), axis=1)

    _run_skewed(groups + [gate_stages()])
    out_ref[0] = g_s[...] * _dot(yn_s[...], wbr_ref[...])


def _ssd_branch(x, norm_w, w_z, w_xbc, w_dt, w_dtT, w_gate, conv_w, conv_b, dt_bias, dt_biasT,
                a_log, a_logT, d_skip_x, expand2, ssm_norm_w, w_br):
    bsz, s, _ = x.shape
    ts = min(SSD_SEQ_TILE, s)
    assert ts % SSD_CHUNK == 0 and SSD_CHUNK % CUMSUM_BLOCK == 0
    tile = lambda: pl.BlockSpec((1, ts, D_MODEL), lambda b, i: (b, i, 0))
    consts = [norm_w, w_z, w_xbc, w_dt, w_dtT, w_gate, conv_w, conv_b, dt_bias, dt_biasT,
              a_log, a_logT, d_skip_x, expand2, ssm_norm_w, w_br]
    return pl.pallas_call(
        _ssd_kernel,
        grid=(bsz, s // ts),
        in_specs=[tile()] + [_const_spec(c.shape) for c in consts],
        out_specs=tile(),
        out_shape=jax.ShapeDtypeStruct(x.shape, F32),
        scratch_shapes=[pltpu.VMEM((SSM_CONV_DIM // LANES, ts + SUBLANES, LANES), F32),
                        pltpu.VMEM((ts, D_MODEL), BF16),
                        pltpu.VMEM((ts, SSM_INNER), BF16),
                        pltpu.VMEM((ts, D_MODEL), F32),
                        pltpu.VMEM((SSM_GROUPS, SSM_STATE, SSM_GROUP_W), F32)],
        compiler_params=_params(2),
        name="ssd_branch",
    )(x, *consts)


def _group_xbc_columns():
    cols = []
    for gi in range(SSM_GROUPS):
        cols += list(range(gi * SSM_GROUP_W, (gi + 1) * SSM_GROUP_W))
        cols += list(range(SSM_INNER + gi * SSM_STATE, SSM_INNER + (gi + 1) * SSM_STATE))
        cols += list(range(SSM_INNER + SSM_BC + gi * SSM_STATE, SSM_INNER + SSM_BC + (gi + 1) * SSM_STATE))
    return jnp.asarray(cols, dtype=jnp.int32)


def _xattn_kernel(x_ref, mhg_ref, mssm_ref, k_ref, v_ref, nw_ref, wq_ref, wg_ref, wbr_ref,
                  wout_ref, npost_ref, out_ref, o_s):
    x = x_ref[0]
    h = _rmsnorm(x, nw_ref[...]).astype(BF16)
    q = _dot(h, wq_ref[...]).astype(BF16)
    for z in range(XA_HEADS):
        hs = slice(z * XA_HEADDIM, (z + 1) * XA_HEADDIM)
        sc = _dot_nt(q[:, hs], k_ref[0, :, hs]) * (XA_HEADDIM ** -0.5)
        e = jnp.exp(sc - jnp.max(sc, axis=-1, keepdims=True))
        p = e / jnp.sum(e, axis=-1, keepdims=True)
        o_s[:, hs] = _dot(p.astype(BF16), v_ref[0, :, hs]).astype(BF16)
    gate = _sigmoid(_dot(h, wg_ref[...]))
    merged = mhg_ref[0] + mssm_ref[0] + gate * _dot(o_s[...], wbr_ref[...])
    out_ref[0] = x + _rmsnorm(_dot(merged.astype(BF16), wout_ref[...]), npost_ref[...])


def _xattn_merge(x, m_hg, m_ssm, k, v, norm_w, w_q, w_gate, w_br, w_out, norm_post):
    bsz, s, _ = x.shape
    ts = min(XA_SEQ_TILE, s)
    tile = lambda: pl.BlockSpec((1, ts, D_MODEL), lambda b, i: (b, i, 0))
    kv = lambda: pl.BlockSpec((1, N_MEM, XA_WIDTH), lambda b, i: (b, 0, 0))
    sq = lambda: _const_spec((D_MODEL, D_MODEL))
    return pl.pallas_call(
        _xattn_kernel,
        grid=(bsz, s // ts),
        in_specs=[tile(), tile(), tile(), kv(), kv(), _const_spec((1, D_MODEL)),
                  sq(), sq(), sq(), sq(), _const_spec((1, D_MODEL))],
        out_specs=tile(),
        out_shape=jax.ShapeDtypeStruct(x.shape, F32),
        scratch_shapes=[pltpu.VMEM((ts, XA_WIDTH), BF16)],
        compiler_params=_params(2),
        name="xattn_merge",
    )(x, m_hg, m_ssm, k, v, norm_w, w_q, w_gate, w_br, w_out, norm_post)


def _ffn_kernel(x_ref, npre_ref, wup_ref, wdown_ref, npost_ref, out_ref):
    x = x_ref[...]
    h = _rmsnorm(x, npre_ref[...]).astype(BF16)
    g = _dot(h, wup_ref[:, :FFN_HIDDEN])
    u = _dot(h, wup_ref[:, FFN_HIDDEN:])
    act = (_silu(g) * u).astype(BF16)
    out_ref[...] = x + _rmsnorm(_dot(act, wdown_ref[...]), npost_ref[...])


def _ffn(x2d, norm_pre, w_up, w_down, norm_post):
    rows = x2d.shape[0]
    tile = min(ROW_TILE, rows)
    spec = lambda: pl.BlockSpec((tile, D_MODEL), lambda i: (i, 0))
    return pl.pallas_call(
        _ffn_kernel,
        grid=(rows // tile,),
        in_specs=[spec(), _const_spec((1, D_MODEL)), _const_spec((D_MODEL, 2 * FFN_HIDDEN)),
                  _const_spec((FFN_HIDDEN, D_MODEL)), _const_spec((1, D_MODEL))],
        out_specs=spec(),
        out_shape=jax.ShapeDtypeStruct(x2d.shape, F32),
        compiler_params=_params(1),
        name="swiglu_ffn",
    )(x2d, norm_pre, w_up, w_down, norm_post)


def _pad_heads(v, fill=0.0):
    row = jnp.full((1, HEAD_PAD), fill, F32).at[0, :SSM_HEADS].set(v.astype(F32))
    return row, row.reshape(HEAD_PAD, 1)


def kernel(x, mem, norm_mix_pre, norm_mix_post, norm_ffn_pre, norm_ffn_post, norm_mem, w_in,
           hg_lb_param, hg_norm_w, conv_w, conv_b, dt_bias, a_log, d_skip, ssm_norm_w, w_mem_kv,
           w_br_hg, w_br_ssm, w_br_xa, w_out, w_ffn_up, w_ffn_down):
    bsz, s, d = x.shape
    depth = w_in.shape[0]
    assert d == D_MODEL and all(s % min(t, s) == 0 for t in (HG_SEQ_TILE, SSD_SEQ_TILE, XA_SEQ_TILE))
    row = lambda v: v.reshape(1, -1).astype(F32)

    expand = (jnp.arange(HEAD_PAD)[:, None] == (jnp.arange(SSM_INNER)[None, :] // SSM_HEADDIM)).astype(BF16)
    expand2 = jnp.concatenate([expand, expand], axis=0)

    xbc_cols = _group_xbc_columns()
    o_hg = 4 * HG_WIDTH
    o_z, o_xbc = o_hg, o_hg + SSM_INNER
    o_dt = o_xbc + SSM_CONV_DIM
    o_q = o_dt + SSM_HEADS
    o_gate = o_q + XA_WIDTH

    for l in range(depth):
        w = w_in[l]
        w_hg = w[:, :o_hg].astype(BF16)
        w_z = w[:, o_z:o_xbc].astype(BF16)
        w_xbc = w[:, o_xbc:o_dt][:, xbc_cols].astype(BF16)
        w_dt = jnp.zeros((D_MODEL, HEAD_PAD), BF16).at[:, :SSM_HEADS].set(w[:, o_dt:o_q].astype(BF16))
        w_q = w[:, o_q:o_gate].astype(BF16)
        w_g = [w[:, o_gate + i * D_MODEL:o_gate + (i + 1) * D_MODEL].astype(BF16) for i in range(3)]
        dtb, dtbT = _pad_heads(dt_bias[l])
        alog, alogT = _pad_heads(a_log[l])
        d_skip_x = jnp.repeat(d_skip[l].astype(F32), SSM_HEADDIM).reshape(1, SSM_INNER)
        n_pre = row(norm_mix_pre[l])

        k, v = _mem_kv(mem.reshape(-1, D_MODEL), row(norm_mem[l]), w_mem_kv[l].astype(BF16))
        k = k.reshape(bsz, -1, XA_WIDTH)
        v = v.reshape(bsz, -1, XA_WIDTH)
        m_hg = _hgrn2_branch(l, x, n_pre, w_hg, w_g[0], hg_lb_param.astype(F32),
                             row(hg_norm_w[l]), w_br_hg[l].astype(BF16))
        m_ssm = _ssd_branch(x, n_pre, w_z, w_xbc, w_dt, w_dt.T, w_g[1], conv_w[l].astype(F32)[:, xbc_cols],
                            row(conv_b[l])[:, xbc_cols], dtb, dtbT, alog, alogT, d_skip_x, expand2,
                            row(ssm_norm_w[l]), w_br_ssm[l].astype(BF16))
        x = _xattn_merge(x, m_hg, m_ssm, k, v, n_pre, w_q, w_g[2], w_br_xa[l].astype(BF16),
                         w_out[l].astype(BF16), row(norm_mix_post[l]))
        x = _ffn(x.reshape(-1, D_MODEL), row(norm_ffn_pre[l]), w_ffn_up[l].astype(BF16),
                 w_ffn_down[l].astype(BF16), row(norm_ffn_post[l])).reshape(bsz, s, d)
    return x
```

```python
import functools

import jax
import jax.numpy as jnp
from jax import lax
from jax.experimental import pallas as pl
from jax.experimental.pallas import tpu as pltpu

F32 = jnp.float32
BF16 = jnp.bfloat16

EPS = 1e-6
D_MODEL = 1024
N_MEM = 256

HG_HEADS = 8
HG_DK = 128
HG_DV = 128
HG_WIDTH = HG_HEADS * HG_DV
HG_CHUNK = 32

SSM_INNER = 2048
SSM_HEADDIM = 64
SSM_HEADS = 32
SSM_STATE = 128
SSM_GROUPS = 8
SSM_HPG = 4
SSM_CONV = 4
SSM_BC = SSM_GROUPS * SSM_STATE
SSM_CONV_DIM = SSM_INNER + 2 * SSM_BC
SSM_GROUP_W = SSM_HPG * SSM_HEADDIM
LANES = 128
SUBLANES = 8
MXU_DEPTH = 256
HEAD_PAD = LANES

XA_HEADS = 4
XA_HEADDIM = 256
XA_WIDTH = 1024

FFN_HIDDEN = 2816

VMEM_LIMIT_BYTES = 56 * 1024 * 1024

HG_SEQ_TILE = 512
SSD_SEQ_TILE = 512
XA_SEQ_TILE = 512
ROW_TILE = 512
HG_SUBTILE = 128
HG_COL_BLOCK = 256
SSD_CHUNK = 128
CUMSUM_BLOCK = 128
SSD_EARLY_GROUPS = 3


def _dot(a, b):
    return jnp.dot(a, b, preferred_element_type=F32)


def _dot_nt(a, b):
    return lax.dot_general(a, b, (((1,), (1,)), ((), ())), preferred_element_type=F32)


def _dot_tn(a, b):
    return lax.dot_general(a, b, (((0,), (0,)), ((), ())), preferred_element_type=F32)


def _split_bf16(v):
    hi = v.astype(BF16)
    return hi, (v - hi.astype(F32)).astype(BF16)


def _rmsnorm(x, w):
    return x * lax.rsqrt(jnp.mean(x * x, axis=-1, keepdims=True) + EPS) * w


def _sigmoid(x):
    return 1.0 / (1.0 + jnp.exp(-x))


def _silu(x):
    return x * _sigmoid(x)


def _softplus(x):
    return jnp.maximum(x, 0.0) + jnp.log1p(jnp.exp(-jnp.abs(x)))


def _chunk_tri(n, chunk, lower):
    r = lax.broadcasted_iota(jnp.int32, (n, n), 0)
    c = lax.broadcasted_iota(jnp.int32, (n, n), 1)
    same = (r // chunk) == (c // chunk)
    tri = (c <= r) if lower else (r <= c)
    return jnp.where(same & tri, 1.0, 0.0).astype(BF16)


def _chunk_cumsum_rows(v, chunk):
    blk = max(CUMSUM_BLOCK, chunk)
    tri = _chunk_tri(blk, chunk, True)
    hi, lo = _split_bf16(v)
    if 2 * blk > MXU_DEPTH:
        return jnp.concatenate(
            [_dot(tri, hi[r0:r0 + blk]) + _dot(tri, lo[r0:r0 + blk]) for r0 in range(0, v.shape[0], blk)], axis=0)
    tri2 = jnp.concatenate([tri, tri], axis=1)
    return jnp.concatenate(
        [_dot(tri2, jnp.concatenate([hi[r0:r0 + blk], lo[r0:r0 + blk]], axis=0))
         for r0 in range(0, v.shape[0], blk)], axis=0)


def _chunk_cumsum_lanes(v, chunk):
    blk = max(CUMSUM_BLOCK, chunk)
    tri = _chunk_tri(blk, chunk, False)
    hi, lo = _split_bf16(v)
    if 2 * blk > MXU_DEPTH:
        return jnp.concatenate(
            [_dot(hi[:, c0:c0 + blk], tri) + _dot(lo[:, c0:c0 + blk], tri) for c0 in range(0, v.shape[1], blk)], axis=1)
    tri2 = jnp.concatenate([tri, tri], axis=0)
    return jnp.concatenate(
        [_dot(jnp.concatenate([hi[:, c0:c0 + blk], lo[:, c0:c0 + blk]], axis=1), tri2)
         for c0 in range(0, v.shape[1], blk)], axis=1)


def _const_spec(shape):
    nd = len(shape)
    return pl.BlockSpec(shape, lambda *_: (0,) * nd, pipeline_mode=pl.Buffered(1))


def _params(n_grid):
    return pltpu.CompilerParams(dimension_semantics=("arbitrary",) * n_grid,
                                vmem_limit_bytes=VMEM_LIMIT_BYTES)


def _mem_kv_kernel(mem_ref, nw_ref, w_ref, k_ref, v_ref):
    mn = _rmsnorm(mem_ref[...], nw_ref[...]).astype(BF16)
    kv = _dot(mn, w_ref[...])
    k_ref[...] = kv[:, :XA_WIDTH].astype(BF16)
    v_ref[...] = kv[:, XA_WIDTH:].astype(BF16)


def _mem_kv(mem2d, norm_w, w_kv):
    rows = mem2d.shape[0]
    tile = min(ROW_TILE, rows)
    out = jax.ShapeDtypeStruct((rows, XA_WIDTH), BF16)
    return pl.pallas_call(
        _mem_kv_kernel,
        grid=(rows // tile,),
        in_specs=[pl.BlockSpec((tile, D_MODEL), lambda i: (i, 0)),
                  _const_spec((1, D_MODEL)),
                  _const_spec((D_MODEL, 2 * XA_WIDTH))],
        out_specs=[pl.BlockSpec((tile, XA_WIDTH), lambda i: (i, 0))] * 2,
        out_shape=[out, out],
        compiler_params=_params(1),
        name="mem_kv",
    )(mem2d, norm_w, w_kv)


_DONE = object()


def _run_skewed(stage_generators):
    waiting = list(stage_generators)
    active = []
    while waiting or active:
        if waiting:
            active.append(waiting.pop(0))
        active = [g for g in active if next(g, _DONE) is not _DONE]


def _hgrn2_kernel(layer, x_ref, nw_ref, w_ref, wg_ref, lbp_ref, gnw_ref, wbr_ref, out_ref,
                  qr_s, kr_s, qi_s, ki_s, v_s, y_s, h_s, o_s, g_s, dch_s, state_s):
    ts = x_ref.shape[1]
    n_chunk = ts // HG_CHUNK
    sub = min(HG_SUBTILE, ts)
    width = HG_COL_BLOCK
    assert n_chunk % 2 == 0 and ts % sub == 0 and ts % CUMSUM_BLOCK == 0 and width % HG_DK == 0

    @pl.when(pl.program_id(1) == 0)
    def _():
        state_s[...] = jnp.zeros_like(state_s)

    h_s[...] = _rmsnorm(x_ref[0], nw_ref[...]).astype(BF16)

    lbp = lbp_ref[...]
    lbe = jnp.exp(lbp - jnp.max(lbp, axis=0, keepdims=True))
    lb_all = jnp.sum(lbe[0:layer + 1], axis=0, keepdims=True) / jnp.sum(lbe, axis=0, keepdims=True)

    tri = _chunk_tri(CUMSUM_BLOCK, HG_CHUNK, True)
    tri2 = jnp.concatenate([tri, tri], axis=1)
    r = lax.broadcasted_iota(jnp.int32, (sub, sub), 0)
    c = lax.broadcasted_iota(jnp.int32, (sub, sub), 1)
    causal = ((r // HG_CHUNK) == (c // HG_CHUNK)) & (c <= r)
    zeros = jnp.zeros((HG_CHUNK, HG_DK), BF16)
    subtiles = [slice(r0, r0 + sub) for r0 in range(0, ts, sub)]
    row_blocks = [slice(r0, r0 + CUMSUM_BLOCK) for r0 in range(0, ts, CUMSUM_BLOCK)]
    chunk_rows = [slice(ci * HG_CHUNK, (ci + 1) * HG_CHUNK) for ci in range(n_chunk)]

    def block_stages(c0):
        cb = slice(c0, c0 + width)
        wcol = lambda section: slice(section * HG_WIDTH + c0, section * HG_WIDTH + c0 + width)
        heads = [slice(c0 + i * HG_DK, c0 + (i + 1) * HG_DK) for i in range(width // HG_DK)]
        pf = _dot(h_s[...], w_ref[:, wcol(1)])
        pq = _dot(h_s[...], w_ref[:, wcol(0)])
        yield
        lb = lb_all[:, cb]
        f = lb + (1.0 - lb) * _sigmoid(pf)
        k = 1.0 - f
        hi, lo = _split_bf16(jnp.log(f))
        yield
        b = jnp.concatenate([_dot(tri2, jnp.concatenate([hi[rb], lo[rb]], axis=0)) for rb in row_blocks], axis=0)
        v_s[:, cb] = _dot(h_s[...], w_ref[:, wcol(2)]).astype(BF16)
        og = _dot(h_s[...], w_ref[:, wcol(3)])
        yield
        b3 = b.reshape(n_chunk, HG_CHUNK, width)
        b_ref = b3[:, HG_CHUNK // 2:HG_CHUNK // 2 + 1]
        b_last = b3[:, HG_CHUNK - 1:HG_CHUNK]
        kr3 = k.reshape(n_chunk, HG_CHUNK, width) * jnp.exp(b_ref - b3)
        kr_s[:, cb] = kr3.reshape(ts, width).astype(BF16)
        ki_s[:, cb] = (kr3 * jnp.exp(b_last - b_ref)).reshape(ts, width).astype(BF16)
        dch_s[:, cb] = jnp.exp(b_last).reshape(n_chunk, width)
        qr3 = _silu(pq).reshape(n_chunk, HG_CHUNK, width) * jnp.exp(b3 - b_ref)
        qr_s[:, cb] = qr3.reshape(ts, width).astype(BF16)
        qi_s[:, cb] = (qr3 * jnp.exp(b_ref)).reshape(ts, width).astype(BF16)
        o_s[:, cb] = _silu(og) * gnw_ref[:, cb]
        yield
        scores, incs = [], []
        for hs in heads:
            scores.append([jnp.where(causal, _dot_nt(qr_s[rs, hs], kr_s[rs, hs]), 0.0).astype(BF16)
                           for rs in subtiles])
            inc_h = []
            for pi in range(n_chunk // 2):
                ra, rb = chunk_rows[2 * pi], chunk_rows[2 * pi + 1]
                vab = jnp.concatenate([jnp.concatenate([v_s[ra, hs], zeros], axis=1),
                                       jnp.concatenate([zeros, v_s[rb, hs]], axis=1)], axis=0)
                inc = _dot_tn(ki_s[ra.start:rb.stop, hs], vab)
                inc_h += [inc[:, :HG_DV], inc[:, HG_DV:]]
            incs.append(inc_h)
        yield
        o_intra, states = [], []
        for hi_, hs in enumerate(heads):
            o_intra.append([_dot(a, v_s[rs, hs]) for a, rs in zip(scores[hi_], subtiles)])
            decay_t = jnp.concatenate([dch_s[:, hs], jnp.zeros((HG_DK - n_chunk, HG_DK), F32)], axis=0).T
            z = hs.start // HG_DK
            st = state_s[z]
            st_h = []
            for ci in range(n_chunk):
                st_h.append(st.astype(BF16))
                st = decay_t[:, ci:ci + 1] * st + incs[hi_][ci]
            state_s[z] = st
            states.append(st_h)
        yield
        for hi_, hs in enumerate(heads):
            o_inter = []
            for pi in range(n_chunk // 2):
                ra, rb = chunk_rows[2 * pi], chunk_rows[2 * pi + 1]
                pair = _dot(qi_s[ra.start:rb.stop, hs],
                            jnp.concatenate([states[hi_][2 * pi], states[hi_][2 * pi + 1]], axis=1))
                o_inter += [pair[:HG_CHUNK, :HG_DV], pair[HG_CHUNK:, HG_DV:]]
            o = jnp.concatenate(o_intra[hi_], axis=0) + jnp.concatenate(o_inter, axis=0)
            y_s[:, hs] = (o * lax.rsqrt(jnp.mean(o * o, axis=-1, keepdims=True) + EPS) * o_s[:, hs]).astype(BF16)

    def gate_stages():
        for c0 in range(0, D_MODEL, width):
            g_s[:, c0:c0 + width] = _sigmoid(_dot(h_s[...], wg_ref[:, c0:c0 + width]))
            yield

    _run_skewed([block_stages(c0) for c0 in range(0, HG_WIDTH, width)] + [gate_stages()])
    out_ref[0] = g_s[...] * _dot(y_s[...], wbr_ref[...])


def _hgrn2_branch(layer, x, norm_w, w_hg, w_gate, lb_param, gn_w, w_br):
    bsz, s, _ = x.shape
    ts = min(HG_SEQ_TILE, s)
    tile = lambda: pl.BlockSpec((1, ts, D_MODEL), lambda b, i: (b, i, 0))
    return pl.pallas_call(
        functools.partial(_hgrn2_kernel, layer),
        grid=(bsz, s // ts),
        in_specs=[tile(),
                  _const_spec((1, D_MODEL)),
                  _const_spec((D_MODEL, 4 * HG_WIDTH)),
                  _const_spec((D_MODEL, D_MODEL)),
                  _const_spec(lb_param.shape),
                  _const_spec((1, HG_WIDTH)),
                  _const_spec((HG_WIDTH, D_MODEL))],
        out_specs=tile(),
        out_shape=jax.ShapeDtypeStruct(x.shape, F32),
        scratch_shapes=[pltpu.VMEM((ts, HG_WIDTH), BF16)] * 7
                       + [pltpu.VMEM((ts, HG_WIDTH), F32)] * 2
                       + [pltpu.VMEM((ts // HG_CHUNK, HG_WIDTH), F32),
                          pltpu.VMEM((HG_HEADS, HG_DV, HG_DK), F32)],
        compiler_params=_params(2),
        name="hgrn2_branch",
    )(x, norm_w, w_hg, w_gate, lb_param, gn_w, w_br)


def _ssd_kernel(x_ref, nw_ref, wz_ref, wxbc_ref, wdt_ref, wdtT_ref, wg_ref, convw_ref, convb_ref,
                dtb_ref, dtbT_ref, a_ref, aT_ref, dskip_ref, expand2_ref, normw_ref, wbr_ref, out_ref,
                raw_s, h_s, yn_s, g_s, state_s):
    ts = x_ref.shape[1]
    lc = SSD_CHUNK
    n_chunk = ts // lc
    pad = SUBLANES
    gw = SSM_GROUP_W + 2 * SSM_STATE
    slabs_per_group = gw // LANES

    @pl.when(pl.program_id(1) == 0)
    def _():
        state_s[...] = jnp.zeros_like(state_s)
        raw_s[:, 0:pad, :] = jnp.zeros((SSM_CONV_DIM // LANES, pad, LANES), F32)

    h_s[...] = _rmsnorm(x_ref[0], nw_ref[...]).astype(BF16)

    r = lax.broadcasted_iota(jnp.int32, (lc, lc), 0)
    c = lax.broadcasted_iota(jnp.int32, (lc, lc), 1)
    causal = c <= r
    head_in_group = lax.broadcasted_iota(jnp.int32, (1, SSM_GROUP_W), 1) // SSM_HEADDIM
    chunks = [slice(ci * lc, (ci + 1) * lc) for ci in range(n_chunk)]

    def group_stages(gi):
        ws = slice(gi * SSM_GROUP_W, (gi + 1) * SSM_GROUP_W)
        pcols = slice(gi * gw, (gi + 1) * gw)
        raw = _dot(h_s[...], wxbc_ref[:, pcols])
        yield
        heads = range(gi * SSM_HPG, (gi + 1) * SSM_HPG)
        din = jnp.concatenate([jnp.broadcast_to(din_f[:, hd:hd + 1], (ts, SSM_HEADDIM)) for hd in heads], axis=1)
        dtd = jnp.concatenate([jnp.broadcast_to(dtd_f[:, hd:hd + 1], (ts, SSM_HEADDIM)) for hd in heads], axis=1)
        cols = []
        for si in range(slabs_per_group):
            ct = gi * slabs_per_group + si
            ls = slice(si * LANES, (si + 1) * LANES)
            pls = slice(gi * gw + si * LANES, gi * gw + (si + 1) * LANES)
            raw_s[ct, pad:pad + ts, :] = raw[:, ls]
            acc = convb_ref[:, pls] + convw_ref[SSM_CONV - 1:SSM_CONV, pls] * raw[:, ls]
            for j in range(SSM_CONV - 1):
                off = pad - (SSM_CONV - 1) + j
                acc = acc + convw_ref[j:j + 1, pls] * raw_s[ct, off:off + ts, :]
            raw_s[ct, 0:pad, :] = raw_s[ct, ts:ts + pad, :]
            cols.append(_silu(acc))
        xs = jnp.concatenate(cols[:SSM_GROUP_W // LANES], axis=1)
        bm = cols[-2].astype(BF16)
        cm = cols[-1].astype(BF16)
        xd = (xs * dtd).astype(BF16)
        xs_bf = xs.astype(BF16)
        y0 = dskip_ref[:, ws] * xs
        yield
        incs = [_dot_tn(bm[rs], xd[rs]) for rs in chunks]
        cbs = [_dot_nt(cm[rs], bm[rs]) for rs in chunks]
        zp = _dot(h_s[...], wz_ref[:, ws])
        yield
        st = state_s[gi]
        states = []
        for ci in range(n_chunk):
            states.append(st.astype(BF16))
            st = din[ci * lc + lc - 1:ci * lc + lc] * st + incs[ci]
        state_s[gi] = st
        ms = []
        for ci, rs in enumerate(chunks):
            for j in range(SSM_HPG):
                hd = gi * SSM_HPG + j
                seg = cum[rs, hd:hd + 1] - cumT[hd:hd + 1, rs]
                ms.append(jnp.where(causal, cbs[ci] * jnp.exp(seg) * dtT[hd:hd + 1, rs], 0.0).astype(BF16))
        yield
        ys = []
        for ci, rs in enumerate(chunks):
            m_all = jnp.concatenate(ms[ci * SSM_HPG:(ci + 1) * SSM_HPG], axis=1)
            x_all = jnp.concatenate([jnp.where(head_in_group == j, xs_bf[rs], jnp.zeros_like(xs_bf[rs]))
                                     for j in range(SSM_HPG)], axis=0)
            ys.append(_dot(cm[rs], states[ci]) * din[rs] + _dot(m_all, x_all))
        yield
        y = (y0 + jnp.concatenate(ys, axis=0)) * _silu(zp)
        yn_s[:, ws] = (y * lax.rsqrt(jnp.mean(y * y, axis=-1, keepdims=True) + EPS) * normw_ref[:, ws]).astype(BF16)

    def gate_stages():
        for c0 in range(0, D_MODEL, SSM_GROUP_W):
            g_s[:, c0:c0 + SSM_GROUP_W] = _sigmoid(_dot(h_s[...], wg_ref[:, c0:c0 + SSM_GROUP_W]))
            yield

    groups = [group_stages(gi) for gi in range(SSM_GROUPS)]
    for gen in groups[:SSD_EARLY_GROUPS]:
        next(gen)

    a = -jnp.exp(a_ref[...])
    dt = _softplus(_dot(h_s[...], wdt_ref[...]) + dtb_ref[...])
    cum = _chunk_cumsum_rows(dt * a, lc)
    aT = -jnp.exp(aT_ref[...])
    dtT = _softplus(_dot_nt(wdtT_ref[...], h_s[...]) + dtbT_ref[...])
    cumT = _chunk_cumsum_lanes(dtT * aT, lc)
    cum3 = cum.reshape(n_chunk, lc, HEAD_PAD)
    dout = jnp.exp(cum3[:, lc - 1:lc] - cum3).reshape(ts, HEAD_PAD)
    din_f = jnp.exp(cum)
    dtd_f = dt * dout

    _run_skewed(groups + [gate_stages()])
    out_ref[0] = g_s[...] * _dot(yn_s[...], wbr_ref[...])


def _ssd_branch(x, norm_w, w_z, w_xbc, w_dt, w_dtT, w_gate, conv_w, conv_b, dt_bias, dt_biasT,
                a_log, a_logT, d_skip_x, expand2, ssm_norm_w, w_br):
    bsz, s, _ = x.shape
    ts = min(SSD_SEQ_TILE, s)
    assert ts % SSD_CHUNK == 0 and SSD_CHUNK % CUMSUM_BLOCK == 0
    tile = lambda: pl.BlockSpec((1, ts, D_MODEL), lambda b, i: (b, i, 0))
    consts = [norm_w, w_z, w_xbc, w_dt, w_dtT, w_gate, conv_w, conv_b, dt_bias, dt_biasT,
              a_log, a_logT, d_skip_x, expand2, ssm_norm_w, w_br]
    return pl.pallas_call(
        _ssd_kernel,
        grid=(bsz, s // ts),
        in_specs=[tile()] + [_const_spec(c.shape) for c in consts],
        out_specs=tile(),
        out_shape=jax.ShapeDtypeStruct(x.shape, F32),
        scratch_shapes=[pltpu.VMEM((SSM_CONV_DIM // LANES, ts + SUBLANES, LANES), F32),
                        pltpu.VMEM((ts, D_MODEL), BF16),
                        pltpu.VMEM((ts, SSM_INNER), BF16),
                        pltpu.VMEM((ts, D_MODEL), F32),
                        pltpu.VMEM((SSM_GROUPS, SSM_STATE, SSM_GROUP_W), F32)],
        compiler_params=_params(2),
        name="ssd_branch",
    )(x, *consts)


def _group_xbc_columns():
    cols = []
    for gi in range(SSM_GROUPS):
        cols += list(range(gi * SSM_GROUP_W, (gi + 1) * SSM_GROUP_W))
        cols += list(range(SSM_INNER + gi * SSM_STATE, SSM_INNER + (gi + 1) * SSM_STATE))
        cols += list(range(SSM_INNER + SSM_BC + gi * SSM_STATE, SSM_INNER + SSM_BC + (gi + 1) * SSM_STATE))
    return jnp.asarray(cols, dtype=jnp.int32)


def _xattn_kernel(x_ref, mhg_ref, mssm_ref, k_ref, v_ref, nw_ref, wq_ref, wg_ref, wbr_ref,
                  wout_ref, npost_ref, out_ref, o_s):
    x = x_ref[0]
    h = _rmsnorm(x, nw_ref[...]).astype(BF16)
    q = _dot(h, wq_ref[...]).astype(BF16)
    for z in range(XA_HEADS):
        hs = slice(z * XA_HEADDIM, (z + 1) * XA_HEADDIM)
        sc = _dot_nt(q[:, hs], k_ref[0, :, hs]) * (XA_HEADDIM ** -0.5)
        e = jnp.exp(sc - jnp.max(sc, axis=-1, keepdims=True))
        p = e / jnp.sum(e, axis=-1, keepdims=True)
        o_s[:, hs] = _dot(p.astype(BF16), v_ref[0, :, hs]).astype(BF16)
    gate = _sigmoid(_dot(h, wg_ref[...]))
    merged = mhg_ref[0] + mssm_ref[0] + gate * _dot(o_s[...], wbr_ref[...])
    out_ref[0] = x + _rmsnorm(_dot(merged.astype(BF16), wout_ref[...]), npost_ref[...])


def _xattn_merge(x, m_hg, m_ssm, k, v, norm_w, w_q, w_gate, w_br, w_out, norm_post):
    bsz, s, _ = x.shape
    ts = min(XA_SEQ_TILE, s)
    tile = lambda: pl.BlockSpec((1, ts, D_MODEL), lambda b, i: (b, i, 0))
    kv = lambda: pl.BlockSpec((1, N_MEM, XA_WIDTH), lambda b, i: (b, 0, 0))
    sq = lambda: _const_spec((D_MODEL, D_MODEL))
    return pl.pallas_call(
        _xattn_kernel,
        grid=(bsz, s // ts),
        in_specs=[tile(), tile(), tile(), kv(), kv(), _const_spec((1, D_MODEL)),
                  sq(), sq(), sq(), sq(), _const_spec((1, D_MODEL))],
        out_specs=tile(),
        out_shape=jax.ShapeDtypeStruct(x.shape, F32),
        scratch_shapes=[pltpu.VMEM((ts, XA_WIDTH), BF16)],
        compiler_params=_params(2),
        name="xattn_merge",
    )(x, m_hg, m_ssm, k, v, norm_w, w_q, w_gate, w_br, w_out, norm_post)


def _ffn_kernel(x_ref, npre_ref, wup_ref, wdown_ref, npost_ref, out_ref):
    x = x_ref[...]
    h = _rmsnorm(x, npre_ref[...]).astype(BF16)
    g = _dot(h, wup_ref[:, :FFN_HIDDEN])
    u = _dot(h, wup_ref[:, FFN_HIDDEN:])
    act = (_silu(g) * u).astype(BF16)
    out_ref[...] = x + _rmsnorm(_dot(act, wdown_ref[...]), npost_ref[...])


def _ffn(x2d, norm_pre, w_up, w_down, norm_post):
    rows = x2d.shape[0]
    tile = min(ROW_TILE, rows)
    spec = lambda: pl.BlockSpec((tile, D_MODEL), lambda i: (i, 0))
    return pl.pallas_call(
        _ffn_kernel,
        grid=(rows // tile,),
        in_specs=[spec(), _const_spec((1, D_MODEL)), _const_spec((D_MODEL, 2 * FFN_HIDDEN)),
                  _const_spec((FFN_HIDDEN, D_MODEL)), _const_spec((1, D_MODEL))],
        out_specs=spec(),
        out_shape=jax.ShapeDtypeStruct(x2d.shape, F32),
        compiler_params=_params(1),
        name="swiglu_ffn",
    )(x2d, norm_pre, w_up, w_down, norm_post)


def _pad_heads(v, fill=0.0):
    row = jnp.full((1, HEAD_PAD), fill, F32).at[0, :SSM_HEADS].set(v.astype(F32))
    return row, row.reshape(HEAD_PAD, 1)


def kernel(x, mem, norm_mix_pre, norm_mix_post, norm_ffn_pre, norm_ffn_post, norm_mem, w_in,
           hg_lb_param, hg_norm_w, conv_w, conv_b, dt_bias, a_log, d_skip, ssm_norm_w, w_mem_kv,
           w_br_hg, w_br_ssm, w_br_xa, w_out, w_ffn_up, w_ffn_down):
    bsz, s, d = x.shape
    depth = w_in.shape[0]
    assert d == D_MODEL and all(s % min(t, s) == 0 for t in (HG_SEQ_TILE, SSD_SEQ_TILE, XA_SEQ_TILE))
    row = lambda v: v.reshape(1, -1).astype(F32)

    expand = (jnp.arange(HEAD_PAD)[:, None] == (jnp.arange(SSM_INNER)[None, :] // SSM_HEADDIM)).astype(BF16)
    expand2 = jnp.concatenate([expand, expand], axis=0)

    xbc_cols = _group_xbc_columns()
    o_hg = 4 * HG_WIDTH
    o_z, o_xbc = o_hg, o_hg + SSM_INNER
    o_dt = o_xbc + SSM_CONV_DIM
    o_q = o_dt + SSM_HEADS
    o_gate = o_q + XA_WIDTH

    for l in range(depth):
        w = w_in[l]
        w_hg = w[:, :o_hg].astype(BF16)
        w_z = w[:, o_z:o_xbc].astype(BF16)
        w_xbc = w[:, o_xbc:o_dt][:, xbc_cols].astype(BF16)
        w_dt = jnp.zeros((D_MODEL, HEAD_PAD), BF16).at[:, :SSM_HEADS].set(w[:, o_dt:o_q].astype(BF16))
        w_q = w[:, o_q:o_gate].astype(BF16)
        w_g = [w[:, o_gate + i * D_MODEL:o_gate + (i + 1) * D_MODEL].astype(BF16) for i in range(3)]
        dtb, dtbT = _pad_heads(dt_bias[l])
        alog, alogT = _pad_heads(a_log[l])
        d_skip_x = jnp.repeat(d_skip[l].astype(F32), SSM_HEADDIM).reshape(1, SSM_INNER)
        n_pre = row(norm_mix_pre[l])

        k, v = _mem_kv(mem.reshape(-1, D_MODEL), row(norm_mem[l]), w_mem_kv[l].astype(BF16))
        k = k.reshape(bsz, -1, XA_WIDTH)
        v = v.reshape(bsz, -1, XA_WIDTH)
        m_hg = _hgrn2_branch(l, x, n_pre, w_hg, w_g[0], hg_lb_param.astype(F32),
                             row(hg_norm_w[l]), w_br_hg[l].astype(BF16))
        m_ssm = _ssd_branch(x, n_pre, w_z, w_xbc, w_dt, w_dt.T, w_g[1], conv_w[l].astype(F32)[:, xbc_cols],
                            row(conv_b[l])[:, xbc_cols], dtb, dtbT, alog, alogT, d_skip_x, expand2,
                            row(ssm_norm_w[l]), w_br_ssm[l].astype(BF16))
        x = _xattn_merge(x, m_hg, m_ssm, k, v, n_pre, w_q, w_g[2], w_br_xa[l].astype(BF16),
                         w_out[l].astype(BF16), row(norm_mix_post[l]))
        x = _ffn(x.reshape(-1, D_MODEL), row(norm_ffn_pre[l]), w_ffn_up[l].astype(BF16),
                 w_ffn_down[l].astype(BF16), row(norm_ffn_post[l])).reshape(bsz, s, d)
    return x
```

```python
import functools

import jax
import jax.numpy as jnp
from jax import lax
from jax.experimental import pallas as pl
from jax.experimental.pallas import tpu as pltpu

F32 = jnp.float32
BF16 = jnp.bfloat16

EPS = 1e-6
D_MODEL = 1024
N_MEM = 256

HG_HEADS = 8
HG_DK = 128
HG_DV = 128
HG_WIDTH = HG_HEADS * HG_DV
HG_CHUNK = 32

SSM_INNER = 2048
SSM_HEADDIM = 64
SSM_HEADS = 32
SSM_STATE = 128
SSM_GROUPS = 8
SSM_HPG = 4
SSM_CONV = 4
SSM_BC = SSM_GROUPS * SSM_STATE
SSM_CONV_DIM = SSM_INNER + 2 * SSM_BC
SSM_GROUP_W = SSM_HPG * SSM_HEADDIM
LANES = 128
SUBLANES = 8
MXU_DEPTH = 256
HEAD_PAD = LANES

XA_HEADS = 4
XA_HEADDIM = 256
XA_WIDTH = 1024

FFN_HIDDEN = 2816

VMEM_LIMIT_BYTES = 56 * 1024 * 1024

HG_SEQ_TILE = 512
SSD_SEQ_TILE = 512
XA_SEQ_TILE = 512
ROW_TILE = 512
HG_SUBTILE = 128
HG_COL_BLOCK = 256
SSD_CHUNK = 128
CUMSUM_BLOCK = 128
SSD_EARLY_GROUPS = 3


def _dot(a, b):
    return jnp.dot(a, b, preferred_element_type=F32)


def _dot_nt(a, b):
    return lax.dot_general(a, b, (((1,), (1,)), ((), ())), preferred_element_type=F32)


def _dot_tn(a, b):
    return lax.dot_general(a, b, (((0,), (0,)), ((), ())), preferred_element_type=F32)


def _split_bf16(v):
    hi = v.astype(BF16)
    return hi, (v - hi.astype(F32)).astype(BF16)


def _rmsnorm(x, w):
    return x * lax.rsqrt(jnp.mean(x * x, axis=-1, keepdims=True) + EPS) * w


def _sigmoid(x):
    return 1.0 / (1.0 + jnp.exp(-x))


def _silu(x):
    return x * _sigmoid(x)


def _softplus(x):
    return jnp.maximum(x, 0.0) + jnp.log1p(jnp.exp(-jnp.abs(x)))


def _chunk_tri(n, chunk, lower):
    r = lax.broadcasted_iota(jnp.int32, (n, n), 0)
    c = lax.broadcasted_iota(jnp.int32, (n, n), 1)
    same = (r // chunk) == (c // chunk)
    tri = (c <= r) if lower else (r <= c)
    return jnp.where(same & tri, 1.0, 0.0).astype(BF16)


def _chunk_cumsum_rows(v, chunk):
    blk = max(CUMSUM_BLOCK, chunk)
    tri = _chunk_tri(blk, chunk, True)
    hi, lo = _split_bf16(v)
    if 2 * blk > MXU_DEPTH:
        return jnp.concatenate(
            [_dot(tri, hi[r0:r0 + blk]) + _dot(tri, lo[r0:r0 + blk]) for r0 in range(0, v.shape[0], blk)], axis=0)
    tri2 = jnp.concatenate([tri, tri], axis=1)
    return jnp.concatenate(
        [_dot(tri2, jnp.concatenate([hi[r0:r0 + blk], lo[r0:r0 + blk]], axis=0))
         for r0 in range(0, v.shape[0], blk)], axis=0)


def _chunk_cumsum_lanes(v, chunk):
    blk = max(CUMSUM_BLOCK, chunk)
    tri = _chunk_tri(blk, chunk, False)
    hi, lo = _split_bf16(v)
    if 2 * blk > MXU_DEPTH:
        return jnp.concatenate(
            [_dot(hi[:, c0:c0 + blk], tri) + _dot(lo[:, c0:c0 + blk], tri) for c0 in range(0, v.shape[1], blk)], axis=1)
    tri2 = jnp.concatenate([tri, tri], axis=0)
    return jnp.concatenate(
        [_dot(jnp.concatenate([hi[:, c0:c0 + blk], lo[:, c0:c0 + blk]], axis=1), tri2)
         for c0 in range(0, v.shape[1], blk)], axis=1)


def _const_spec(shape):
    nd = len(shape)
    return pl.BlockSpec(shape, lambda *_: (0,) * nd, pipeline_mode=pl.Buffered(1))


def _params(n_grid):
    return pltpu.CompilerParams(dimension_semantics=("arbitrary",) * n_grid,
                                vmem_limit_bytes=VMEM_LIMIT_BYTES)


def _mem_kv_kernel(mem_ref, nw_ref, w_ref, k_ref, v_ref):
    mn = _rmsnorm(mem_ref[...], nw_ref[...]).astype(BF16)
    kv = _dot(mn, w_ref[...])
    k_ref[...] = kv[:, :XA_WIDTH].astype(BF16)
    v_ref[...] = kv[:, XA_WIDTH:].astype(BF16)


def _mem_kv(mem2d, norm_w, w_kv):
    rows = mem2d.shape[0]
    tile = min(ROW_TILE, rows)
    out = jax.ShapeDtypeStruct((rows, XA_WIDTH), BF16)
    return pl.pallas_call(
        _mem_kv_kernel,
        grid=(rows // tile,),
        in_specs=[pl.BlockSpec((tile, D_MODEL), lambda i: (i, 0)),
                  _const_spec((1, D_MODEL)),
                  _const_spec((D_MODEL, 2 * XA_WIDTH))],
        out_specs=[pl.BlockSpec((tile, XA_WIDTH), lambda i: (i, 0))] * 2,
        out_shape=[out, out],
        compiler_params=_params(1),
        name="mem_kv",
    )(mem2d, norm_w, w_kv)


_DONE = object()


def _run_skewed(stage_generators):
    waiting = list(stage_generators)
    active = []
    while waiting or active:
        if waiting:
            active.append(waiting.pop(0))
        active = [g for g in active if next(g, _DONE) is not _DONE]


def _hgrn2_kernel(layer, x_ref, nw_ref, w_ref, wg_ref, lbp_ref, gnw_ref, wbr_ref, out_ref,
                  qr_s, kr_s, qi_s, ki_s, v_s, y_s, h_s, o_s, g_s, dch_s, state_s):
    ts = x_ref.shape[1]
    n_chunk = ts // HG_CHUNK
    sub = min(HG_SUBTILE, ts)
    width = HG_COL_BLOCK
    assert n_chunk % 2 == 0 and ts % sub == 0 and ts % CUMSUM_BLOCK == 0 and width % HG_DK == 0

    @pl.when(pl.program_id(1) == 0)
    def _():
        state_s[...] = jnp.zeros_like(state_s)

    h_s[...] = _rmsnorm(x_ref[0], nw_ref[...]).astype(BF16)

    lbp = lbp_ref[...]
    lbe = jnp.exp(lbp - jnp.max(lbp, axis=0, keepdims=True))
    lb_all = jnp.sum(lbe[0:layer + 1], axis=0, keepdims=True) / jnp.sum(lbe, axis=0, keepdims=True)

    tri = _chunk_tri(CUMSUM_BLOCK, HG_CHUNK, True)
    tri2 = jnp.concatenate([tri, tri], axis=1)
    r = lax.broadcasted_iota(jnp.int32, (sub, sub), 0)
    c = lax.broadcasted_iota(jnp.int32, (sub, sub), 1)
    causal = ((r // HG_CHUNK) == (c // HG_CHUNK)) & (c <= r)
    zeros = jnp.zeros((HG_CHUNK, HG_DK), BF16)
    subtiles = [slice(r0, r0 + sub) for r0 in range(0, ts, sub)]
    row_blocks = [slice(r0, r0 + CUMSUM_BLOCK) for r0 in range(0, ts, CUMSUM_BLOCK)]
    chunk_rows = [slice(ci * HG_CHUNK, (ci + 1) * HG_CHUNK) for ci in range(n_chunk)]

    def block_stages(c0):
        cb = slice(c0, c0 + width)
        wcol = lambda section: slice(section * HG_WIDTH + c0, section * HG_WIDTH + c0 + width)
        heads = [slice(c0 + i * HG_DK, c0 + (i + 1) * HG_DK) for i in range(width // HG_DK)]
        pf = _dot(h_s[...], w_ref[:, wcol(1)])
        pq = _dot(h_s[...], w_ref[:, wcol(0)])
        yield
        lb = lb_all[:, cb]
        f = lb + (1.0 - lb) * _sigmoid(pf)
        k = 1.0 - f
        hi, lo = _split_bf16(jnp.log(f))
        yield
        b = jnp.concatenate([_dot(tri2, jnp.concatenate([hi[rb], lo[rb]], axis=0)) for rb in row_blocks], axis=0)
        v_s[:, cb] = _dot(h_s[...], w_ref[:, wcol(2)]).astype(BF16)
        og = _dot(h_s[...], w_ref[:, wcol(3)])
        yield
        b3 = b.reshape(n_chunk, HG_CHUNK, width)
        b_ref = b3[:, HG_CHUNK // 2:HG_CHUNK // 2 + 1]
        b_last = b3[:, HG_CHUNK - 1:HG_CHUNK]
        kr3 = k.reshape(n_chunk, HG_CHUNK, width) * jnp.exp(b_ref - b3)
        kr_s[:, cb] = kr3.reshape(ts, width).astype(BF16)
        ki_s[:, cb] = (kr3 * jnp.exp(b_last - b_ref)).reshape(ts, width).astype(BF16)
        dch_s[:, cb] = jnp.exp(b_last).reshape(n_chunk, width)
        qr3 = _silu(pq).reshape(n_chunk, HG_CHUNK, width) * jnp.exp(b3 - b_ref)
        qr_s[:, cb] = qr3.reshape(ts, width).astype(BF16)
        qi_s[:, cb] = (qr3 * jnp.exp(b_ref)).reshape(ts, width).astype(BF16)
        o_s[:, cb] = _silu(og) * gnw_ref[:, cb]
        yield
        scores, incs = [], []
        for hs in heads:
            scores.append([jnp.where(causal, _dot_nt(qr_s[rs, hs], kr_s[rs, hs]), 0.0).astype(BF16)
                           for rs in subtiles])
            inc_h = []
            for pi in range(n_chunk // 2):
                ra, rb = chunk_rows[2 * pi], chunk_rows[2 * pi + 1]
                vab = jnp.concatenate([jnp.concatenate([v_s[ra, hs], zeros], axis=1),
                                       jnp.concatenate([zeros, v_s[rb, hs]], axis=1)], axis=0)
                inc = _dot_tn(ki_s[ra.start:rb.stop, hs], vab)
                inc_h += [inc[:, :HG_DV], inc[:, HG_DV:]]
            incs.append(inc_h)
        yield
        o_intra, states = [], []
        for hi_, hs in enumerate(heads):
            o_intra.append([_dot(a, v_s[rs, hs]) for a, rs in zip(scores[hi_], subtiles)])
            decay_t = jnp.concatenate([dch_s[:, hs], jnp.zeros((HG_DK - n_chunk, HG_DK), F32)], axis=0).T
            z = hs.start // HG_DK
            st = state_s[z]
            st_h = []
            for ci in range(n_chunk):
                st_h.append(st.astype(BF16))
                st = decay_t[:, ci:ci + 1] * st + incs[hi_][ci]
            state_s[z] = st
            states.append(st_h)
        yield
        for hi_, hs in enumerate(heads):
            o_inter = []
            for pi in range(n_chunk // 2):
                ra, rb = chunk_rows[2 * pi], chunk_rows[2 * pi + 1]
                pair = _dot(qi_s[ra.start:rb.stop, hs],
                            jnp.concatenate([states[hi_][2 * pi], states[hi_][2 * pi + 1]], axis=1))
                o_inter += [pair[:HG_CHUNK, :HG_DV], pair[HG_CHUNK:, HG_DV:]]
            o = jnp.concatenate(o_intra[hi_], axis=0) + jnp.concatenate(o_inter, axis=0)
            y_s[:, hs] = (o * lax.rsqrt(jnp.mean(o * o, axis=-1, keepdims=True) + EPS) * o_s[:, hs]).astype(BF16)

    def gate_stages():
        for c0 in range(0, D_MODEL, width):
            g_s[:, c0:c0 + width] = _sigmoid(_dot(h_s[...], wg_ref[:, c0:c0 + width]))
            yield

    _run_skewed([block_stages(c0) for c0 in range(0, HG_WIDTH, width)] + [gate_stages()])
    out_ref[0] = g_s[...] * _dot(y_s[...], wbr_ref[...])


def _hgrn2_branch(layer, x, norm_w, w_hg, w_gate, lb_param, gn_w, w_br):
    bsz, s, _ = x.shape
    ts = min(HG_SEQ_TILE, s)
    tile = lambda: pl.BlockSpec((1, ts, D_MODEL), lambda b, i: (b, i, 0))
    return pl.pallas_call(
        functools.partial(_hgrn2_kernel, layer),
        grid=(bsz, s // ts),
        in_specs=[tile(),
                  _const_spec((1, D_MODEL)),
                  _const_spec((D_MODEL, 4 * HG_WIDTH)),
                  _const_spec((D_MODEL, D_MODEL)),
                  _const_spec(lb_param.shape),
                  _const_spec((1, HG_WIDTH)),
                  _const_spec((HG_WIDTH, D_MODEL))],
        out_specs=tile(),
        out_shape=jax.ShapeDtypeStruct(x.shape, F32),
        scratch_shapes=[pltpu.VMEM((ts, HG_WIDTH), BF16)] * 7
                       + [pltpu.VMEM((ts, HG_WIDTH), F32)] * 2
                       + [pltpu.VMEM((ts // HG_CHUNK, HG_WIDTH), F32),
                          pltpu.VMEM((HG_HEADS, HG_DV, HG_DK), F32)],
        compiler_params=_params(2),
        name="hgrn2_branch",
    )(x, norm_w, w_hg, w_gate, lb_param, gn_w, w_br)


def _ssd_kernel(x_ref, nw_ref, wz_ref, wxbc_ref, wdt_ref, wdtT_ref, wg_ref, convw_ref, convb_ref,
                dtb_ref, dtbT_ref, a_ref, aT_ref, dskip_ref, expand2_ref, normw_ref, wbr_ref, out_ref,
                raw_s, h_s, yn_s, g_s, state_s):
    ts = x_ref.shape[1]
    lc = SSD_CHUNK
    n_chunk = ts // lc
    pad = SUBLANES
    gw = SSM_GROUP_W + 2 * SSM_STATE
    slabs_per_group = gw // LANES

    @pl.when(pl.program_id(1) == 0)
    def _():
        state_s[...] = jnp.zeros_like(state_s)
        raw_s[:, 0:pad, :] = jnp.zeros((SSM_CONV_DIM // LANES, pad, LANES), F32)

    h_s[...] = _rmsnorm(x_ref[0], nw_ref[...]).astype(BF16)

    r = lax.broadcasted_iota(jnp.int32, (lc, lc), 0)
    c = lax.broadcasted_iota(jnp.int32, (lc, lc), 1)
    causal = c <= r
    head_in_group = lax.broadcasted_iota(jnp.int32, (1, SSM_GROUP_W), 1) // SSM_HEADDIM
    chunks = [slice(ci * lc, (ci + 1) * lc) for ci in range(n_chunk)]

    def group_stages(gi):
        ws = slice(gi * SSM_GROUP_W, (gi + 1) * SSM_GROUP_W)
        pcols = slice(gi * gw, (gi + 1) * gw)
        raw = _dot(h_s[...], wxbc_ref[:, pcols])
        yield
        din = _dot(din_hl, expand2_ref[:, ws])
        dtd = _dot(dtd_hl, expand2_ref[:, ws])
        cols = []
        for si in range(slabs_per_group):
            ct = gi * slabs_per_group + si
            ls = slice(si * LANES, (si + 1) * LANES)
            pls = slice(gi * gw + si * LANES, gi * gw + (si + 1) * LANES)
            raw_s[ct, pad:pad + ts, :] = raw[:, ls]
            acc = convb_ref[:, pls] + convw_ref[SSM_CONV - 1:SSM_CONV, pls] * raw[:, ls]
            for j in range(SSM_CONV - 1):
                off = pad - (SSM_CONV - 1) + j
                acc = acc + convw_ref[j:j + 1, pls] * raw_s[ct, off:off + ts, :]
            raw_s[ct, 0:pad, :] = raw_s[ct, ts:ts + pad, :]
            cols.append(_silu(acc))
        xs = jnp.concatenate(cols[:SSM_GROUP_W // LANES], axis=1)
        bm = cols[-2].astype(BF16)
        cm = cols[-1].astype(BF16)
        xd = (xs * dtd).astype(BF16)
        xs_bf = xs.astype(BF16)
        y0 = dskip_ref[:, ws] * xs
        yield
        incs = [_dot_tn(bm[rs], xd[rs]) for rs in chunks]
        cbs = [_dot_nt(cm[rs], bm[rs]) for rs in chunks]
        zp = _dot(h_s[...], wz_ref[:, ws])
        yield
        st = state_s[gi]
        states = []
        for ci in range(n_chunk):
            states.append(st.astype(BF16))
            st = din[ci * lc + lc - 1:ci * lc + lc] * st + incs[ci]
        state_s[gi] = st
        ms = []
        for ci, rs in enumerate(chunks):
            for j in range(SSM_HPG):
                hd = gi * SSM_HPG + j
                seg = cum[rs, hd:hd + 1] - cumT[hd:hd + 1, rs]
                ms.append(jnp.where(causal, cbs[ci] * jnp.exp(seg) * dtT[hd:hd + 1, rs], 0.0).astype(BF16))
        yield
        ys = []
        for ci, rs in enumerate(chunks):
            m_all = jnp.concatenate(ms[ci * SSM_HPG:(ci + 1) * SSM_HPG], axis=1)
            x_all = jnp.concatenate([jnp.where(head_in_group == j, xs_bf[rs], jnp.zeros_like(xs_bf[rs]))
                                     for j in range(SSM_HPG)], axis=0)
            ys.append(_dot(cm[rs], states[ci]) * din[rs] + _dot(m_all, x_all))
        yield
        y = (y0 + jnp.concatenate(ys, axis=0)) * _silu(zp)
        yn_s[:, ws] = (y * lax.rsqrt(jnp.mean(y * y, axis=-1, keepdims=True) + EPS) * normw_ref[:, ws]).astype(BF16)

    def gate_stages():
        for c0 in range(0, D_MODEL, SSM_GROUP_W):
            g_s[:, c0:c0 + SSM_GROUP_W] = _sigmoid(_dot(h_s[...], wg_ref[:, c0:c0 + SSM_GROUP_W]))
            yield

    groups = [group_stages(gi) for gi in range(SSM_GROUPS)]
    for gen in groups[:SSD_EARLY_GROUPS]:
        next(gen)

    a = -jnp.exp(a_ref[...])
    dt = _softplus(_dot(h_s[...], wdt_ref[...]) + dtb_ref[...])
    cum = _chunk_cumsum_rows(dt * a, lc)
    aT = -jnp.exp(aT_ref[...])
    dtT = _softplus(_dot_nt(wdtT_ref[...], h_s[...]) + dtbT_ref[...])
    cumT = _chunk_cumsum_lanes(dtT * aT, lc)
    cum3 = cum.reshape(n_chunk, lc, HEAD_PAD)
    dout = jnp.exp(cum3[:, lc - 1:lc] - cum3).reshape(ts, HEAD_PAD)
    din_hl = jnp.concatenate(_split_bf16(jnp.exp(cum)), axis=1)
    dtd_hl = jnp.concatenate(_split_bf16(dt * dout), axis=1)

    _run_skewed(groups + [gate_stages()])
    out_ref[0] = g_s[...] * _dot(yn_s[...], wbr_ref[...])


def _ssd_branch(x, norm_w, w_z, w_xbc, w_dt, w_dtT, w_gate, conv_w, conv_b, dt_bias, dt_biasT,
                a_log, a_logT, d_skip_x, expand2, ssm_norm_w, w_br):
    bsz, s, _ = x.shape
    ts = min(SSD_SEQ_TILE, s)
    assert ts % SSD_CHUNK == 0 and SSD_CHUNK % CUMSUM_BLOCK == 0
    tile = lambda: pl.BlockSpec((1, ts, D_MODEL), lambda b, i: (b, i, 0))
    consts = [norm_w, w_z, w_xbc, w_dt, w_dtT, w_gate, conv_w, conv_b, dt_bias, dt_biasT,
              a_log, a_logT, d_skip_x, expand2, ssm_norm_w, w_br]
    return pl.pallas_call(
        _ssd_kernel,
        grid=(bsz, s // ts),
        in_specs=[tile()] + [_const_spec(c.shape) for c in consts],
        out_specs=tile(),
        out_shape=jax.ShapeDtypeStruct(x.shape, F32),
        scratch_shapes=[pltpu.VMEM((SSM_CONV_DIM // LANES, ts + SUBLANES, LANES), F32),
                        pltpu.VMEM((ts, D_MODEL), BF16),
                        pltpu.VMEM((ts, SSM_INNER), BF16),
                        pltpu.VMEM((ts, D_MODEL), F32),
                        pltpu.VMEM((SSM_GROUPS, SSM_STATE, SSM_GROUP_W), F32)],
        compiler_params=_params(2),
        name="ssd_branch",
    )(x, *consts)


def _group_xbc_columns(a):
    parts = []
    for gi in range(SSM_GROUPS):
        parts.append(a[..., gi * SSM_GROUP_W:(gi + 1) * SSM_GROUP_W])
        parts.append(a[..., SSM_INNER + gi * SSM_STATE:SSM_INNER + (gi + 1) * SSM_STATE])
        parts.append(a[..., SSM_INNER + SSM_BC + gi * SSM_STATE:SSM_INNER + SSM_BC + (gi + 1) * SSM_STATE])
    return jnp.concatenate(parts, axis=-1)


def _xattn_kernel(x_ref, mhg_ref, mssm_ref, k_ref, v_ref, nw_ref, wq_ref, wg_ref, wbr_ref,
                  wout_ref, npost_ref, out_ref, o_s):
    x = x_ref[0]
    h = _rmsnorm(x, nw_ref[...]).astype(BF16)
    q = _dot(h, wq_ref[...]).astype(BF16)
    for z in range(XA_HEADS):
        hs = slice(z * XA_HEADDIM, (z + 1) * XA_HEADDIM)
        sc = _dot_nt(q[:, hs], k_ref[0, :, hs]) * (XA_HEADDIM ** -0.5)
        e = jnp.exp(sc - jnp.max(sc, axis=-1, keepdims=True))
        p = e / jnp.sum(e, axis=-1, keepdims=True)
        o_s[:, hs] = _dot(p.astype(BF16), v_ref[0, :, hs]).astype(BF16)
    gate = _sigmoid(_dot(h, wg_ref[...]))
    merged = mhg_ref[0] + mssm_ref[0] + gate * _dot(o_s[...], wbr_ref[...])
    out_ref[0] = x + _rmsnorm(_dot(merged.astype(BF16), wout_ref[...]), npost_ref[...])


def _xattn_merge(x, m_hg, m_ssm, k, v, norm_w, w_q, w_gate, w_br, w_out, norm_post):
    bsz, s, _ = x.shape
    ts = min(XA_SEQ_TILE, s)
    tile = lambda: pl.BlockSpec((1, ts, D_MODEL), lambda b, i: (b, i, 0))
    kv = lambda: pl.BlockSpec((1, N_MEM, XA_WIDTH), lambda b, i: (b, 0, 0))
    sq = lambda: _const_spec((D_MODEL, D_MODEL))
    return pl.pallas_call(
        _xattn_kernel,
        grid=(bsz, s // ts),
        in_specs=[tile(), tile(), tile(), kv(), kv(), _const_spec((1, D_MODEL)),
                  sq(), sq(), sq(), sq(), _const_spec((1, D_MODEL))],
        out_specs=tile(),
        out_shape=jax.ShapeDtypeStruct(x.shape, F32),
        scratch_shapes=[pltpu.VMEM((ts, XA_WIDTH), BF16)],
        compiler_params=_params(2),
        name="xattn_merge",
    )(x, m_hg, m_ssm, k, v, norm_w, w_q, w_gate, w_br, w_out, norm_post)


def _ffn_kernel(x_ref, npre_ref, wup_ref, wdown_ref, npost_ref, out_ref):
    x = x_ref[...]
    h = _rmsnorm(x, npre_ref[...]).astype(BF16)
    g = _dot(h, wup_ref[:, :FFN_HIDDEN])
    u = _dot(h, wup_ref[:, FFN_HIDDEN:])
    act = (_silu(g) * u).astype(BF16)
    out_ref[...] = x + _rmsnorm(_dot(act, wdown_ref[...]), npost_ref[...])


def _ffn(x2d, norm_pre, w_up, w_down, norm_post):
    rows = x2d.shape[0]
    tile = min(ROW_TILE, rows)
    spec = lambda: pl.BlockSpec((tile, D_MODEL), lambda i: (i, 0))
    return pl.pallas_call(
        _ffn_kernel,
        grid=(rows // tile,),
        in_specs=[spec(), _const_spec((1, D_MODEL)), _const_spec((D_MODEL, 2 * FFN_HIDDEN)),
                  _const_spec((FFN_HIDDEN, D_MODEL)), _const_spec((1, D_MODEL))],
        out_specs=spec(),
        out_shape=jax.ShapeDtypeStruct(x2d.shape, F32),
        compiler_params=_params(1),
        name="swiglu_ffn",
    )(x2d, norm_pre, w_up, w_down, norm_post)


def _pad_heads(v, fill=0.0):
    row = jnp.full((1, HEAD_PAD), fill, F32).at[0, :SSM_HEADS].set(v.astype(F32))
    return row, row.reshape(HEAD_PAD, 1)


def kernel(x, mem, norm_mix_pre, norm_mix_post, norm_ffn_pre, norm_ffn_post, norm_mem, w_in,
           hg_lb_param, hg_norm_w, conv_w, conv_b, dt_bias, a_log, d_skip, ssm_norm_w, w_mem_kv,
           w_br_hg, w_br_ssm, w_br_xa, w_out, w_ffn_up, w_ffn_down):
    bsz, s, d = x.shape
    depth = w_in.shape[0]
    assert d == D_MODEL and all(s % min(t, s) == 0 for t in (HG_SEQ_TILE, SSD_SEQ_TILE, XA_SEQ_TILE))
    row = lambda v: v.reshape(1, -1).astype(F32)

    expand = (jnp.arange(HEAD_PAD)[:, None] == (jnp.arange(SSM_INNER)[None, :] // SSM_HEADDIM)).astype(BF16)
    expand2 = jnp.concatenate([expand, expand], axis=0)

    o_hg = 4 * HG_WIDTH
    o_z, o_xbc = o_hg, o_hg + SSM_INNER
    o_dt = o_xbc + SSM_CONV_DIM
    o_q = o_dt + SSM_HEADS
    o_gate = o_q + XA_WIDTH

    for l in range(depth):
        w = w_in[l]
        w_hg = w[:, :o_hg].astype(BF16)
        w_z = w[:, o_z:o_xbc].astype(BF16)
        w_xbc = _group_xbc_columns(w[:, o_xbc:o_dt]).astype(BF16)
        w_dt = jnp.zeros((D_MODEL, HEAD_PAD), BF16).at[:, :SSM_HEADS].set(w[:, o_dt:o_q].astype(BF16))
        w_q = w[:, o_q:o_gate].astype(BF16)
        w_g = [w[:, o_gate + i * D_MODEL:o_gate + (i + 1) * D_MODEL].astype(BF16) for i in range(3)]
        dtb, dtbT = _pad_heads(dt_bias[l])
        alog, alogT = _pad_heads(a_log[l])
        d_skip_x = jnp.repeat(d_skip[l].astype(F32), SSM_HEADDIM).reshape(1, SSM_INNER)
        n_pre = row(norm_mix_pre[l])

        k, v = _mem_kv(mem.reshape(-1, D_MODEL), row(norm_mem[l]), w_mem_kv[l].astype(BF16))
        k = k.reshape(bsz, -1, XA_WIDTH)
        v = v.reshape(bsz, -1, XA_WIDTH)
        m_hg = _hgrn2_branch(l, x, n_pre, w_hg, w_g[0], hg_lb_param.astype(F32),
                             row(hg_norm_w[l]), w_br_hg[l].astype(BF16))
        m_ssm = _ssd_branch(x, n_pre, w_z, w_xbc, w_dt, w_dt.T, w_g[1], _group_xbc_columns(conv_w[l].astype(F32)),
                            _group_xbc_columns(row(conv_b[l])), dtb, dtbT, alog, alogT, d_skip_x, expand2,
                            row(ssm_norm_w[l]), w_br_ssm[l].astype(BF16))
        x = _xattn_merge(x, m_hg, m_ssm, k, v, n_pre, w_q, w_g[2], w_br_xa[l].astype(BF16),
                         w_out[l].astype(BF16), row(norm_mix_post[l]))
        x = _ffn(x.reshape(-1, D_MODEL), row(norm_ffn_pre[l]), w_ffn_up[l].astype(BF16),
                 w_ffn_down[l].astype(BF16), row(norm_ffn_post[l])).reshape(bsz, s, d)
    return x
```

```python
import functools

import jax
import jax.numpy as jnp
from jax import lax
from jax.experimental import pallas as pl
from jax.experimental.pallas import tpu as pltpu

F32 = jnp.float32
BF16 = jnp.bfloat16

EPS = 1e-6
D_MODEL = 1024
N_MEM = 256

HG_HEADS = 8
HG_DK = 128
HG_DV = 128
HG_WIDTH = HG_HEADS * HG_DV
HG_CHUNK = 32

SSM_INNER = 2048
SSM_HEADDIM = 64
SSM_HEADS = 32
SSM_STATE = 128
SSM_GROUPS = 8
SSM_HPG = 4
SSM_CONV = 4
SSM_BC = SSM_GROUPS * SSM_STATE
SSM_CONV_DIM = SSM_INNER + 2 * SSM_BC
SSM_GROUP_W = SSM_HPG * SSM_HEADDIM
LANES = 128
SUBLANES = 8
MXU_DEPTH = 256
HEAD_PAD = LANES

XA_HEADS = 4
XA_HEADDIM = 256
XA_WIDTH = 1024

FFN_HIDDEN = 2816

VMEM_LIMIT_BYTES = 56 * 1024 * 1024

HG_SEQ_TILE = 512
SSD_SEQ_TILE = 512
XA_SEQ_TILE = 512
ROW_TILE = 512
HG_SUBTILE = 128
HG_COL_BLOCK = 256
SSD_CHUNK = 128
CUMSUM_BLOCK = 128
SSD_EARLY_GROUPS = 6


def _dot(a, b):
    return jnp.dot(a, b, preferred_element_type=F32)


def _dot_nt(a, b):
    return lax.dot_general(a, b, (((1,), (1,)), ((), ())), preferred_element_type=F32)


def _dot_tn(a, b):
    return lax.dot_general(a, b, (((0,), (0,)), ((), ())), preferred_element_type=F32)


def _split_bf16(v):
    hi = v.astype(BF16)
    return hi, (v - hi.astype(F32)).astype(BF16)


def _rmsnorm(x, w):
    return x * lax.rsqrt(jnp.mean(x * x, axis=-1, keepdims=True) + EPS) * w


def _sigmoid(x):
    return 1.0 / (1.0 + jnp.exp(-x))


def _silu(x):
    return x * _sigmoid(x)


def _softplus(x):
    return jnp.maximum(x, 0.0) + jnp.log1p(jnp.exp(-jnp.abs(x)))


def _chunk_tri(n, chunk, lower):
    r = lax.broadcasted_iota(jnp.int32, (n, n), 0)
    c = lax.broadcasted_iota(jnp.int32, (n, n), 1)
    same = (r // chunk) == (c // chunk)
    tri = (c <= r) if lower else (r <= c)
    return jnp.where(same & tri, 1.0, 0.0).astype(BF16)


def _chunk_cumsum_rows(v, chunk):
    blk = max(CUMSUM_BLOCK, chunk)
    tri = _chunk_tri(blk, chunk, True)
    hi, lo = _split_bf16(v)
    if 2 * blk > MXU_DEPTH:
        return jnp.concatenate(
            [_dot(tri, hi[r0:r0 + blk]) + _dot(tri, lo[r0:r0 + blk]) for r0 in range(0, v.shape[0], blk)], axis=0)
    tri2 = jnp.concatenate([tri, tri], axis=1)
    return jnp.concatenate(
        [_dot(tri2, jnp.concatenate([hi[r0:r0 + blk], lo[r0:r0 + blk]], axis=0))
         for r0 in range(0, v.shape[0], blk)], axis=0)


def _chunk_cumsum_lanes(v, chunk):
    blk = max(CUMSUM_BLOCK, chunk)
    tri = _chunk_tri(blk, chunk, False)
    hi, lo = _split_bf16(v)
    if 2 * blk > MXU_DEPTH:
        return jnp.concatenate(
            [_dot(hi[:, c0:c0 + blk], tri) + _dot(lo[:, c0:c0 + blk], tri) for c0 in range(0, v.shape[1], blk)], axis=1)
    tri2 = jnp.concatenate([tri, tri], axis=0)
    return jnp.concatenate(
        [_dot(jnp.concatenate([hi[:, c0:c0 + blk], lo[:, c0:c0 + blk]], axis=1), tri2)
         for c0 in range(0, v.shape[1], blk)], axis=1)


def _const_spec(shape):
    nd = len(shape)
    return pl.BlockSpec(shape, lambda *_: (0,) * nd, pipeline_mode=pl.Buffered(1))


def _params(n_grid):
    return pltpu.CompilerParams(dimension_semantics=("arbitrary",) * n_grid,
                                vmem_limit_bytes=VMEM_LIMIT_BYTES)


def _mem_kv_kernel(mem_ref, nw_ref, w_ref, k_ref, v_ref):
    mn = _rmsnorm(mem_ref[...], nw_ref[...]).astype(BF16)
    kv = _dot(mn, w_ref[...])
    k_ref[...] = kv[:, :XA_WIDTH].astype(BF16)
    v_ref[...] = kv[:, XA_WIDTH:].astype(BF16)


def _mem_kv(mem2d, norm_w, w_kv):
    rows = mem2d.shape[0]
    tile = min(ROW_TILE, rows)
    out = jax.ShapeDtypeStruct((rows, XA_WIDTH), BF16)
    return pl.pallas_call(
        _mem_kv_kernel,
        grid=(rows // tile,),
        in_specs=[pl.BlockSpec((tile, D_MODEL), lambda i: (i, 0)),
                  _const_spec((1, D_MODEL)),
                  _const_spec((D_MODEL, 2 * XA_WIDTH))],
        out_specs=[pl.BlockSpec((tile, XA_WIDTH), lambda i: (i, 0))] * 2,
        out_shape=[out, out],
        compiler_params=_params(1),
        name="mem_kv",
    )(mem2d, norm_w, w_kv)


_DONE = object()


def _run_skewed(stage_generators):
    waiting = list(stage_generators)
    active = []
    while waiting or active:
        if waiting:
            active.append(waiting.pop(0))
        active = [g for g in active if next(g, _DONE) is not _DONE]


def _hgrn2_kernel(layer, x_ref, nw_ref, w_ref, wg_ref, lbp_ref, gnw_ref, wbr_ref, out_ref,
                  qr_s, kr_s, qi_s, ki_s, v_s, y_s, h_s, o_s, g_s, dch_s, state_s):
    ts = x_ref.shape[1]
    n_chunk = ts // HG_CHUNK
    sub = min(HG_SUBTILE, ts)
    width = HG_COL_BLOCK
    assert n_chunk % 2 == 0 and ts % sub == 0 and ts % CUMSUM_BLOCK == 0 and width % HG_DK == 0

    @pl.when(pl.program_id(1) == 0)
    def _():
        state_s[...] = jnp.zeros_like(state_s)

    h_s[...] = _rmsnorm(x_ref[0], nw_ref[...]).astype(BF16)

    lbp = lbp_ref[...]
    lbe = jnp.exp(lbp - jnp.max(lbp, axis=0, keepdims=True))
    lb_all = jnp.sum(lbe[0:layer + 1], axis=0, keepdims=True) / jnp.sum(lbe, axis=0, keepdims=True)

    tri = _chunk_tri(CUMSUM_BLOCK, HG_CHUNK, True)
    tri2 = jnp.concatenate([tri, tri], axis=1)
    r = lax.broadcasted_iota(jnp.int32, (sub, sub), 0)
    c = lax.broadcasted_iota(jnp.int32, (sub, sub), 1)
    causal = ((r // HG_CHUNK) == (c // HG_CHUNK)) & (c <= r)
    zeros = jnp.zeros((HG_CHUNK, HG_DK), BF16)
    subtiles = [slice(r0, r0 + sub) for r0 in range(0, ts, sub)]
    row_blocks = [slice(r0, r0 + CUMSUM_BLOCK) for r0 in range(0, ts, CUMSUM_BLOCK)]
    chunk_rows = [slice(ci * HG_CHUNK, (ci + 1) * HG_CHUNK) for ci in range(n_chunk)]

    def block_stages(c0):
        cb = slice(c0, c0 + width)
        wcol = lambda section: slice(section * HG_WIDTH + c0, section * HG_WIDTH + c0 + width)
        heads = [slice(c0 + i * HG_DK, c0 + (i + 1) * HG_DK) for i in range(width // HG_DK)]
        pf = _dot(h_s[...], w_ref[:, wcol(1)])
        pq = _dot(h_s[...], w_ref[:, wcol(0)])
        yield
        lb = lb_all[:, cb]
        f = lb + (1.0 - lb) * _sigmoid(pf)
        k = 1.0 - f
        hi, lo = _split_bf16(jnp.log(f))
        yield
        b = jnp.concatenate([_dot(tri2, jnp.concatenate([hi[rb], lo[rb]], axis=0)) for rb in row_blocks], axis=0)
        v_s[:, cb] = _dot(h_s[...], w_ref[:, wcol(2)]).astype(BF16)
        og = _dot(h_s[...], w_ref[:, wcol(3)])
        yield
        b3 = b.reshape(n_chunk, HG_CHUNK, width)
        b_ref = b3[:, HG_CHUNK // 2:HG_CHUNK // 2 + 1]
        b_last = b3[:, HG_CHUNK - 1:HG_CHUNK]
        kr3 = k.reshape(n_chunk, HG_CHUNK, width) * jnp.exp(b_ref - b3)
        kr_s[:, cb] = kr3.reshape(ts, width).astype(BF16)
        ki_s[:, cb] = (kr3 * jnp.exp(b_last - b_ref)).reshape(ts, width).astype(BF16)
        dch_s[:, cb] = jnp.exp(b_last).reshape(n_chunk, width)
        qr3 = _silu(pq).reshape(n_chunk, HG_CHUNK, width) * jnp.exp(b3 - b_ref)
        qr_s[:, cb] = qr3.reshape(ts, width).astype(BF16)
        qi_s[:, cb] = (qr3 * jnp.exp(b_ref)).reshape(ts, width).astype(BF16)
        o_s[:, cb] = _silu(og) * gnw_ref[:, cb]
        yield
        scores, incs = [], []
        for hs in heads:
            scores.append([jnp.where(causal, _dot_nt(qr_s[rs, hs], kr_s[rs, hs]), 0.0).astype(BF16)
                           for rs in subtiles])
            inc_h = []
            for pi in range(n_chunk // 2):
                ra, rb = chunk_rows[2 * pi], chunk_rows[2 * pi + 1]
                vab = jnp.concatenate([jnp.concatenate([v_s[ra, hs], zeros], axis=1),
                                       jnp.concatenate([zeros, v_s[rb, hs]], axis=1)], axis=0)
                inc = _dot_tn(ki_s[ra.start:rb.stop, hs], vab)
                inc_h += [inc[:, :HG_DV], inc[:, HG_DV:]]
            incs.append(inc_h)
        yield
        o_intra, states = [], []
        for hi_, hs in enumerate(heads):
            o_intra.append([_dot(a, v_s[rs, hs]) for a, rs in zip(scores[hi_], subtiles)])
            decay_t = jnp.concatenate([dch_s[:, hs], jnp.zeros((HG_DK - n_chunk, HG_DK), F32)], axis=0).T
            z = hs.start // HG_DK
            st = state_s[z]
            st_h = []
            for ci in range(n_chunk):
                st_h.append(st.astype(BF16))
                st = decay_t[:, ci:ci + 1] * st + incs[hi_][ci]
            state_s[z] = st
            states.append(st_h)
        yield
        for hi_, hs in enumerate(heads):
            o_inter = []
            for pi in range(n_chunk // 2):
                ra, rb = chunk_rows[2 * pi], chunk_rows[2 * pi + 1]
                pair = _dot(qi_s[ra.start:rb.stop, hs],
                            jnp.concatenate([states[hi_][2 * pi], states[hi_][2 * pi + 1]], axis=1))
                o_inter += [pair[:HG_CHUNK, :HG_DV], pair[HG_CHUNK:, HG_DV:]]
            o = jnp.concatenate(o_intra[hi_], axis=0) + jnp.concatenate(o_inter, axis=0)
            y_s[:, hs] = (o * lax.rsqrt(jnp.mean(o * o, axis=-1, keepdims=True) + EPS) * o_s[:, hs]).astype(BF16)

    def gate_stages():
        for c0 in range(0, D_MODEL, width):
            g_s[:, c0:c0 + width] = _sigmoid(_dot(h_s[...], wg_ref[:, c0:c0 + width]))
            yield

    _run_skewed([block_stages(c0) for c0 in range(0, HG_WIDTH, width)] + [gate_stages()])
    out_ref[0] = g_s[...] * _dot(y_s[...], wbr_ref[...])


def _hgrn2_branch(layer, x, norm_w, w_hg, w_gate, lb_param, gn_w, w_br):
    bsz, s, _ = x.shape
    ts = min(HG_SEQ_TILE, s)
    tile = lambda: pl.BlockSpec((1, ts, D_MODEL), lambda b, i: (b, i, 0))
    return pl.pallas_call(
        functools.partial(_hgrn2_kernel, layer),
        grid=(bsz, s // ts),
        in_specs=[tile(),
                  _const_spec((1, D_MODEL)),
                  _const_spec((D_MODEL, 4 * HG_WIDTH)),
                  _const_spec((D_MODEL, D_MODEL)),
                  _const_spec(lb_param.shape),
                  _const_spec((1, HG_WIDTH)),
                  _const_spec((HG_WIDTH, D_MODEL))],
        out_specs=tile(),
        out_shape=jax.ShapeDtypeStruct(x.shape, F32),
        scratch_shapes=[pltpu.VMEM((ts, HG_WIDTH), BF16)] * 7
                       + [pltpu.VMEM((ts, HG_WIDTH), F32)] * 2
                       + [pltpu.VMEM((ts // HG_CHUNK, HG_WIDTH), F32),
                          pltpu.VMEM((HG_HEADS, HG_DV, HG_DK), F32)],
        compiler_params=_params(2),
        name="hgrn2_branch",
    )(x, norm_w, w_hg, w_gate, lb_param, gn_w, w_br)


def _ssd_kernel(x_ref, nw_ref, wz_ref, wxbc_ref, wdt_ref, wdtT_ref, wg_ref, convw_ref, convb_ref,
                dtb_ref, dtbT_ref, a_ref, aT_ref, dskip_ref, expand2_ref, normw_ref, wbr_ref, out_ref,
                raw_s, h_s, yn_s, g_s, state_s):
    ts = x_ref.shape[1]
    lc = SSD_CHUNK
    n_chunk = ts // lc
    pad = SUBLANES
    gw = SSM_GROUP_W + 2 * SSM_STATE
    slabs_per_group = gw // LANES

    @pl.when(pl.program_id(1) == 0)
    def _():
        state_s[...] = jnp.zeros_like(state_s)
        raw_s[:, 0:pad, :] = jnp.zeros((SSM_CONV_DIM // LANES, pad, LANES), F32)

    h_s[...] = _rmsnorm(x_ref[0], nw_ref[...]).astype(BF16)

    r = lax.broadcasted_iota(jnp.int32, (lc, lc), 0)
    c = lax.broadcasted_iota(jnp.int32, (lc, lc), 1)
    causal = c <= r
    head_in_group = lax.broadcasted_iota(jnp.int32, (1, SSM_GROUP_W), 1) // SSM_HEADDIM
    chunks = [slice(ci * lc, (ci + 1) * lc) for ci in range(n_chunk)]

    def group_stages(gi):
        ws = slice(gi * SSM_GROUP_W, (gi + 1) * SSM_GROUP_W)
        pcols = slice(gi * gw, (gi + 1) * gw)
        raw = _dot(h_s[...], wxbc_ref[:, pcols])
        yield
        din = _dot(din_hl, expand2_ref[:, ws])
        dtd = _dot(dtd_hl, expand2_ref[:, ws])
        cols = []
        for si in range(slabs_per_group):
            ct = gi * slabs_per_group + si
            ls = slice(si * LANES, (si + 1) * LANES)
            pls = slice(gi * gw + si * LANES, gi * gw + (si + 1) * LANES)
            raw_s[ct, pad:pad + ts, :] = raw[:, ls]
            acc = convb_ref[:, pls] + convw_ref[SSM_CONV - 1:SSM_CONV, pls] * raw[:, ls]
            for j in range(SSM_CONV - 1):
                off = pad - (SSM_CONV - 1) + j
                acc = acc + convw_ref[j:j + 1, pls] * raw_s[ct, off:off + ts, :]
            raw_s[ct, 0:pad, :] = raw_s[ct, ts:ts + pad, :]
            cols.append(_silu(acc))
        xs = jnp.concatenate(cols[:SSM_GROUP_W // LANES], axis=1)
        bm = cols[-2].astype(BF16)
        cm = cols[-1].astype(BF16)
        xd = (xs * dtd).astype(BF16)
        xs_bf = xs.astype(BF16)
        y0 = dskip_ref[:, ws] * xs
        yield
        incs = [_dot_tn(bm[rs], xd[rs]) for rs in chunks]
        cbs = [_dot_nt(cm[rs], bm[rs]) for rs in chunks]
        zp = _dot(h_s[...], wz_ref[:, ws])
        yield
        st = state_s[gi]
        states = []
        for ci in range(n_chunk):
            states.append(st.astype(BF16))
            st = din[ci * lc + lc - 1:ci * lc + lc] * st + incs[ci]
        state_s[gi] = st
        ms = []
        for ci, rs in enumerate(chunks):
            for j in range(SSM_HPG):
                hd = gi * SSM_HPG + j
                seg = cum[rs, hd:hd + 1] - cumT[hd:hd + 1, rs]
                ms.append(jnp.where(causal, cbs[ci] * jnp.exp(seg) * dtT[hd:hd + 1, rs], 0.0).astype(BF16))
        yield
        ys = []
        for ci, rs in enumerate(chunks):
            m_all = jnp.concatenate(ms[ci * SSM_HPG:(ci + 1) * SSM_HPG], axis=1)
            x_all = jnp.concatenate([jnp.where(head_in_group == j, xs_bf[rs], jnp.zeros_like(xs_bf[rs]))
                                     for j in range(SSM_HPG)], axis=0)
            ys.append(_dot(cm[rs], states[ci]) * din[rs] + _dot(m_all, x_all))
        yield
        y = (y0 + jnp.concatenate(ys, axis=0)) * _silu(zp)
        yn_s[:, ws] = (y * lax.rsqrt(jnp.mean(y * y, axis=-1, keepdims=True) + EPS) * normw_ref[:, ws]).astype(BF16)

    def gate_stages():
        for c0 in range(0, D_MODEL, SSM_GROUP_W):
            g_s[:, c0:c0 + SSM_GROUP_W] = _sigmoid(_dot(h_s[...], wg_ref[:, c0:c0 + SSM_GROUP_W]))
            yield

    groups = [group_stages(gi) for gi in range(SSM_GROUPS)]
    for gen in groups[:SSD_EARLY_GROUPS]:
        next(gen)

    a = -jnp.exp(a_ref[...])
    dt = _softplus(_dot(h_s[...], wdt_ref[...]) + dtb_ref[...])
    cum = _chunk_cumsum_rows(dt * a, lc)
    aT = -jnp.exp(aT_ref[...])
    dtT = _softplus(_dot_nt(wdtT_ref[...], h_s[...]) + dtbT_ref[...])
    cumT = _chunk_cumsum_lanes(dtT * aT, lc)
    cum3 = cum.reshape(n_chunk, lc, HEAD_PAD)
    dout = jnp.exp(cum3[:, lc - 1:lc] - cum3).reshape(ts, HEAD_PAD)
    din_hl = jnp.concatenate(_split_bf16(jnp.exp(cum)), axis=1)
    dtd_hl = jnp.concatenate(_split_bf16(dt * dout), axis=1)

    _run_skewed(groups + [gate_stages()])
    out_ref[0] = g_s[...] * _dot(yn_s[...], wbr_ref[...])


def _ssd_branch(x, norm_w, w_z, w_xbc, w_dt, w_dtT, w_gate, conv_w, conv_b, dt_bias, dt_biasT,
                a_log, a_logT, d_skip_x, expand2, ssm_norm_w, w_br):
    bsz, s, _ = x.shape
    ts = min(SSD_SEQ_TILE, s)
    assert ts % SSD_CHUNK == 0 and SSD_CHUNK % CUMSUM_BLOCK == 0
    tile = lambda: pl.BlockSpec((1, ts, D_MODEL), lambda b, i: (b, i, 0))
    consts = [norm_w, w_z, w_xbc, w_dt, w_dtT, w_gate, conv_w, conv_b, dt_bias, dt_biasT,
              a_log, a_logT, d_skip_x, expand2, ssm_norm_w, w_br]
    return pl.pallas_call(
        _ssd_kernel,
        grid=(bsz, s // ts),
        in_specs=[tile()] + [_const_spec(c.shape) for c in consts],
        out_specs=tile(),
        out_shape=jax.ShapeDtypeStruct(x.shape, F32),
        scratch_shapes=[pltpu.VMEM((SSM_CONV_DIM // LANES, ts + SUBLANES, LANES), F32),
                        pltpu.VMEM((ts, D_MODEL), BF16),
                        pltpu.VMEM((ts, SSM_INNER), BF16),
                        pltpu.VMEM((ts, D_MODEL), F32),
                        pltpu.VMEM((SSM_GROUPS, SSM_STATE, SSM_GROUP_W), F32)],
        compiler_params=_params(2),
        name="ssd_branch",
    )(x, *consts)


def _group_xbc_columns(a):
    parts = []
    for gi in range(SSM_GROUPS):
        parts.append(a[..., gi * SSM_GROUP_W:(gi + 1) * SSM_GROUP_W])
        parts.append(a[..., SSM_INNER + gi * SSM_STATE:SSM_INNER + (gi + 1) * SSM_STATE])
        parts.append(a[..., SSM_INNER + SSM_BC + gi * SSM_STATE:SSM_INNER + SSM_BC + (gi + 1) * SSM_STATE])
    return jnp.concatenate(parts, axis=-1)


def _xattn_kernel(x_ref, mhg_ref, mssm_ref, k_ref, v_ref, nw_ref, wq_ref, wg_ref, wbr_ref,
                  wout_ref, npost_ref, out_ref, o_s):
    x = x_ref[0]
    h = _rmsnorm(x, nw_ref[...]).astype(BF16)
    q = _dot(h, wq_ref[...]).astype(BF16)
    for z in range(XA_HEADS):
        hs = slice(z * XA_HEADDIM, (z + 1) * XA_HEADDIM)
        sc = _dot_nt(q[:, hs], k_ref[0, :, hs]) * (XA_HEADDIM ** -0.5)
        e = jnp.exp(sc - jnp.max(sc, axis=-1, keepdims=True))
        p = e / jnp.sum(e, axis=-1, keepdims=True)
        o_s[:, hs] = _dot(p.astype(BF16), v_ref[0, :, hs]).astype(BF16)
    gate = _sigmoid(_dot(h, wg_ref[...]))
    merged = mhg_ref[0] + mssm_ref[0] + gate * _dot(o_s[...], wbr_ref[...])
    out_ref[0] = x + _rmsnorm(_dot(merged.astype(BF16), wout_ref[...]), npost_ref[...])


def _xattn_merge(x, m_hg, m_ssm, k, v, norm_w, w_q, w_gate, w_br, w_out, norm_post):
    bsz, s, _ = x.shape
    ts = min(XA_SEQ_TILE, s)
    tile = lambda: pl.BlockSpec((1, ts, D_MODEL), lambda b, i: (b, i, 0))
    kv = lambda: pl.BlockSpec((1, N_MEM, XA_WIDTH), lambda b, i: (b, 0, 0))
    sq = lambda: _const_spec((D_MODEL, D_MODEL))
    return pl.pallas_call(
        _xattn_kernel,
        grid=(bsz, s // ts),
        in_specs=[tile(), tile(), tile(), kv(), kv(), _const_spec((1, D_MODEL)),
                  sq(), sq(), sq(), sq(), _const_spec((1, D_MODEL))],
        out_specs=tile(),
        out_shape=jax.ShapeDtypeStruct(x.shape, F32),
        scratch_shapes=[pltpu.VMEM((ts, XA_WIDTH), BF16)],
        compiler_params=_params(2),
        name="xattn_merge",
    )(x, m_hg, m_ssm, k, v, norm_w, w_q, w_gate, w_br, w_out, norm_post)


def _ffn_kernel(x_ref, npre_ref, wup_ref, wdown_ref, npost_ref, out_ref):
    x = x_ref[...]
    h = _rmsnorm(x, npre_ref[...]).astype(BF16)
    g = _dot(h, wup_ref[:, :FFN_HIDDEN])
    u = _dot(h, wup_ref[:, FFN_HIDDEN:])
    act = (_silu(g) * u).astype(BF16)
    out_ref[...] = x + _rmsnorm(_dot(act, wdown_ref[...]), npost_ref[...])


def _ffn(x2d, norm_pre, w_up, w_down, norm_post):
    rows = x2d.shape[0]
    tile = min(ROW_TILE, rows)
    spec = lambda: pl.BlockSpec((tile, D_MODEL), lambda i: (i, 0))
    return pl.pallas_call(
        _ffn_kernel,
        grid=(rows // tile,),
        in_specs=[spec(), _const_spec((1, D_MODEL)), _const_spec((D_MODEL, 2 * FFN_HIDDEN)),
                  _const_spec((FFN_HIDDEN, D_MODEL)), _const_spec((1, D_MODEL))],
        out_specs=spec(),
        out_shape=jax.ShapeDtypeStruct(x2d.shape, F32),
        compiler_params=_params(1),
        name="swiglu_ffn",
    )(x2d, norm_pre, w_up, w_down, norm_post)


def _pad_heads(v, fill=0.0):
    row = jnp.full((1, HEAD_PAD), fill, F32).at[0, :SSM_HEADS].set(v.astype(F32))
    return row, row.reshape(HEAD_PAD, 1)


def kernel(x, mem, norm_mix_pre, norm_mix_post, norm_ffn_pre, norm_ffn_post, norm_mem, w_in,
           hg_lb_param, hg_norm_w, conv_w, conv_b, dt_bias, a_log, d_skip, ssm_norm_w, w_mem_kv,
           w_br_hg, w_br_ssm, w_br_xa, w_out, w_ffn_up, w_ffn_down):
    bsz, s, d = x.shape
    depth = w_in.shape[0]
    assert d == D_MODEL and all(s % min(t, s) == 0 for t in (HG_SEQ_TILE, SSD_SEQ_TILE, XA_SEQ_TILE))
    row = lambda v: v.reshape(1, -1).astype(F32)

    expand = (jnp.arange(HEAD_PAD)[:, None] == (jnp.arange(SSM_INNER)[None, :] // SSM_HEADDIM)).astype(BF16)
    expand2 = jnp.concatenate([expand, expand], axis=0)

    o_hg = 4 * HG_WIDTH
    o_z, o_xbc = o_hg, o_hg + SSM_INNER
    o_dt = o_xbc + SSM_CONV_DIM
    o_q = o_dt + SSM_HEADS
    o_gate = o_q + XA_WIDTH

    for l in range(depth):
        w = w_in[l]
        w_hg = w[:, :o_hg].astype(BF16)
        w_z = w[:, o_z:o_xbc].astype(BF16)
        w_xbc = _group_xbc_columns(w[:, o_xbc:o_dt]).astype(BF16)
        w_dt = jnp.zeros((D_MODEL, HEAD_PAD), BF16).at[:, :SSM_HEADS].set(w[:, o_dt:o_q].astype(BF16))
        w_q = w[:, o_q:o_gate].astype(BF16)
        w_g = [w[:, o_gate + i * D_MODEL:o_gate + (i + 1) * D_MODEL].astype(BF16) for i in range(3)]
        dtb, dtbT = _pad_heads(dt_bias[l])
        alog, alogT = _pad_heads(a_log[l])
        d_skip_x = jnp.repeat(d_skip[l].astype(F32), SSM_HEADDIM).reshape(1, SSM_INNER)
        n_pre = row(norm_mix_pre[l])

        k, v = _mem_kv(mem.reshape(-1, D_MODEL), row(norm_mem[l]), w_mem_kv[l].astype(BF16))
        k = k.reshape(bsz, -1, XA_WIDTH)
        v = v.reshape(bsz, -1, XA_WIDTH)
        m_hg = _hgrn2_branch(l, x, n_pre, w_hg, w_g[0], hg_lb_param.astype(F32),
                             row(hg_norm_w[l]), w_br_hg[l].astype(BF16))
        m_ssm = _ssd_branch(x, n_pre, w_z, w_xbc, w_dt, w_dt.T, w_g[1], _group_xbc_columns(conv_w[l].astype(F32)),
                            _group_xbc_columns(row(conv_b[l])), dtb, dtbT, alog, alogT, d_skip_x, expand2,
                            row(ssm_norm_w[l]), w_br_ssm[l].astype(BF16))
        x = _xattn_merge(x, m_hg, m_ssm, k, v, n_pre, w_q, w_g[2], w_br_xa[l].astype(BF16),
                         w_out[l].astype(BF16), row(norm_mix_post[l]))
        x = _ffn(x.reshape(-1, D_MODEL), row(norm_ffn_pre[l]), w_ffn_up[l].astype(BF16),
                 w_ffn_down[l].astype(BF16), row(norm_ffn_post[l])).reshape(bsz, s, d)
    return x
```
